```python
import jax, jax.numpy as jnp
from jax import lax
import numpy as np

D_MODEL = 1024
BATCH = 4
SEQ = 4096
DEPTH = 4
DEC_BATCH = 2
DEC_SEQ = 16384
PAST_LEN = 128

GRID_W = 64
D_CONV = 1024
CONV_W = 3
NA_HEADS = 32
NA_HEAD_DIM = 32
D_ATT = NA_HEADS * NA_HEAD_DIM
NA_KH_MAX = 8
NA_KW = 16
N_BRANCH = 2
D_FF = 2816
RMS_EPS = 1e-6
NEG_INF = -1e30
D_IN = 3 * D_CONV + 3 * D_ATT + N_BRANCH * D_MODEL
SPLITS = (D_CONV, 2 * D_CONV, 3 * D_CONV, 3 * D_CONV + D_ATT, 3 * D_CONV + 2 * D_ATT, 3 * D_CONV + 3 * D_ATT)

kernel_name = 'hybrid_shortconv_natten_macaron_encoder'


def rms_norm(x, g):
    xf = x.astype(jnp.float32)
    y = xf * lax.rsqrt(jnp.mean(xf * xf, axis=-1, keepdims=True) + RMS_EPS)
    return (y * g.astype(jnp.float32)).astype(x.dtype)


def swiglu_ffn(x, w_in, w_out):
    g, u = jnp.split(x @ w_in, 2, axis=-1)
    return (jax.nn.silu(g) * u) @ w_out


def short_conv_mixer(b_gate, c_gate, v, conv_w):
    T = v.shape[1]
    z = c_gate * v
    pad = CONV_W // 2
    zp = jnp.pad(z, ((0, 0), (pad, CONV_W - 1 - pad), (0, 0)))
    conv = zp[:, 0:T] * conv_w[0]
    for j in range(1, CONV_W):
        conv = conv + zp[:, j:j + T] * conv_w[j]
    return b_gate * conv


def neighborhood_attention(q, k, v, rpb):
    bsz, T = q.shape[0], q.shape[1]
    rows = T // GRID_W
    kh = min(NA_KH_MAX, rows)
    q = q.reshape(bsz, rows, GRID_W, NA_HEADS, NA_HEAD_DIM)
    k = k.reshape(bsz, rows, GRID_W, NA_HEADS, NA_HEAD_DIM)
    v = v.reshape(bsz, rows, GRID_W, NA_HEADS, NA_HEAD_DIM)
    qc = jnp.arange(GRID_W)
    kc = jnp.arange(GRID_W)
    c_start = jnp.clip(qc - NA_KW // 2, 0, GRID_W - NA_KW)
    col_ok = (kc[None, :] >= c_start[:, None]) & (kc[None, :] < c_start[:, None] + NA_KW)
    dc = jnp.clip(kc[None, :] - qc[:, None], -(NA_KW - 1), NA_KW - 1) + NA_KW - 1
    mask = jnp.broadcast_to(col_ok[:, None, :], (GRID_W, kh, GRID_W)).reshape(GRID_W, kh * GRID_W)
    scale = NA_HEAD_DIM ** -0.5

    def row_step(r):
        r_start = jnp.clip(r - kh // 2, 0, rows - kh)
        q_r = lax.dynamic_index_in_dim(q, r, axis=1, keepdims=False)
        k_blk = lax.dynamic_slice_in_dim(k, r_start, kh, axis=1).reshape(bsz, kh * GRID_W, NA_HEADS, NA_HEAD_DIM)
        v_blk = lax.dynamic_slice_in_dim(v, r_start, kh, axis=1).reshape(bsz, kh * GRID_W, NA_HEADS, NA_HEAD_DIM)
        dr = r_start + jnp.arange(kh) - r + NA_KH_MAX - 1
        bias = rpb[:, dr[:, None, None], dc[None, :, :]]
        bias = bias.transpose(0, 2, 1, 3).reshape(NA_HEADS, GRID_W, kh * GRID_W)
        s = jnp.einsum('bqhd,bkhd->bhqk', q_r, k_blk).astype(jnp.float32) * scale + bias.astype(jnp.float32)
        s = jnp.where(mask, s, NEG_INF)
        p = jax.nn.softmax(s, axis=-1).astype(v_blk.dtype)
        return jnp.einsum('bhqk,bkhd->bqhd', p, v_blk)

    out = lax.map(row_step, jnp.arange(rows))
    return out.transpose(1, 0, 2, 3, 4).reshape(bsz, T, D_ATT)


def encoder_layer(x, g_ffn1_pre, g_ffn1_post, w_ffn1_in, w_ffn1_out,
                  g_mix_pre, g_mix_post, w_mix_in, b_mix_gate, conv_w, na_rpb,
                  w_conv_branch, w_att_branch, w_mix_out,
                  g_ffn2_pre, g_ffn2_post, w_ffn2_in, w_ffn2_out):
    bsz, T, _ = x.shape
    x = x + 0.5 * rms_norm(swiglu_ffn(rms_norm(x, g_ffn1_pre), w_ffn1_in, w_ffn1_out), g_ffn1_post)
    u = rms_norm(x, g_mix_pre)
    proj = u @ w_mix_in
    cb, cc, cv, q, k, v, gate_logits = jnp.split(proj, SPLITS, axis=-1)
    y_conv = short_conv_mixer(cb, cc, cv, conv_w) @ w_conv_branch
    heads = lambda t: t.reshape(bsz, T, NA_HEADS, NA_HEAD_DIM)
    y_att = neighborhood_attention(heads(q), heads(k), heads(v), na_rpb) @ w_att_branch
    gates = jax.nn.sigmoid((gate_logits + b_mix_gate).astype(jnp.float32)).astype(x.dtype)
    g_conv, g_att = jnp.split(gates, 2, axis=-1)
    mixed = (g_conv * y_conv + g_att * y_att) @ w_mix_out
    x = x + rms_norm(mixed, g_mix_post)
    x = x + 0.5 * rms_norm(swiglu_ffn(rms_norm(x, g_ffn2_pre), w_ffn2_in, w_ffn2_out), g_ffn2_post)
    return x


def setup_inputs(seed: int = 0) -> dict:
    key = jax.random.key(seed)
    ks = jax.random.split(key, 24)
    f32 = jnp.float32

    def w(k, shape, fan_in):
        return jax.random.normal(k, shape, f32) * (fan_in ** -0.5)

    def gain(k):
        return 1.0 + 0.02 * jax.random.normal(k, (DEPTH, D_MODEL), f32)

    return {
        'x_prompt': jax.random.normal(ks[0], (BATCH, SEQ, D_MODEL), f32),
        'x_sample': jax.random.normal(ks[1], (DEC_BATCH, DEC_SEQ, D_MODEL), f32),
        'g_ffn1_pre': gain(ks[2]),
        'g_ffn1_post': gain(ks[3]),
        'w_ffn1_in': w(ks[4], (DEPTH, D_MODEL, 2 * D_FF), D_MODEL),
        'w_ffn1_out': w(ks[5], (DEPTH, D_FF, D_MODEL), D_FF),
        'g_mix_pre': gain(ks[6]),
        'g_mix_post': gain(ks[7]),
        'w_mix_in': w(ks[8], (DEPTH, D_MODEL, D_IN), D_MODEL),
        'b_mix_gate': 0.02 * jax.random.normal(ks[9], (DEPTH, N_BRANCH * D_MODEL), f32),
        'conv_w': w(ks[10], (DEPTH, CONV_W, D_CONV), CONV_W),
        'na_rpb': 0.02 * jax.random.normal(ks[11], (DEPTH, NA_HEADS, 2 * NA_KH_MAX - 1, 2 * NA_KW - 1), f32),
        'w_conv_branch': w(ks[12], (DEPTH, D_CONV, D_MODEL), D_CONV),
        'w_att_branch': w(ks[13], (DEPTH, D_ATT, D_MODEL), D_ATT),
        'w_mix_out': w(ks[14], (DEPTH, D_MODEL, D_MODEL), D_MODEL),
        'g_ffn2_pre': gain(ks[15]),
        'g_ffn2_post': gain(ks[16]),
        'w_ffn2_in': w(ks[17], (DEPTH, D_MODEL, 2 * D_FF), D_MODEL),
        'w_ffn2_out': w(ks[18], (DEPTH, D_FF, D_MODEL), D_FF),
    }


def reference(x_prompt, x_sample, g_ffn1_pre, g_ffn1_post, w_ffn1_in, w_ffn1_out,
              g_mix_pre, g_mix_post, w_mix_in, b_mix_gate, conv_w, na_rpb,
              w_conv_branch, w_att_branch, w_mix_out,
              g_ffn2_pre, g_ffn2_post, w_ffn2_in, w_ffn2_out):
    y_prompt = x_prompt
    y_sample = x_sample
    for l in range(DEPTH):
        layer = lambda h: encoder_layer(
            h, g_ffn1_pre[l], g_ffn1_post[l], w_ffn1_in[l], w_ffn1_out[l],
            g_mix_pre[l], g_mix_post[l], w_mix_in[l], b_mix_gate[l], conv_w[l], na_rpb[l],
            w_conv_branch[l], w_att_branch[l], w_mix_out[l],
            g_ffn2_pre[l], g_ffn2_post[l], w_ffn2_in[l], w_ffn2_out[l])
        y_prompt = layer(y_prompt)
        y_sample = layer(y_sample)
    return (y_prompt, y_sample)
```

```python
import functools

import jax
import jax.numpy as jnp
from jax import lax
from jax.experimental import pallas as pl
from jax.experimental.pallas import tpu as pltpu

D_MODEL = 1024
GRID_W = 64
D_CONV = 1024
CONV_W = 3
NA_HEADS = 32
NA_HEAD_DIM = 32
D_ATT = NA_HEADS * NA_HEAD_DIM
NA_KH = 8
NA_KW = 16
D_FF = 2816
RMS_EPS = 1e-6
NEG_INF = -1e30

LANES = 128
HEADS_PER_GROUP = LANES // NA_HEAD_DIM
N_GROUPS = NA_HEADS // HEADS_PER_GROUP
WIN_TOKENS = NA_KH * GRID_W
HALO_ROWS = 4
HALO_TOKENS = HALO_ROWS * GRID_W
CONV_HALO = 8

FFN_CHUNKS = 2
FFN_CHUNK = D_FF // FFN_CHUNKS
TM = 512
ATT_ROWS = 16
VMEM_LIMIT = 56 * 1024 * 1024

F32 = jnp.float32
BF16 = jnp.bfloat16


def _rms(xf, g):
    y = xf * lax.rsqrt(jnp.mean(xf * xf, axis=-1, keepdims=True) + RMS_EPS)
    return y * g


def _resident(shape):
    nd = len(shape)
    return pl.BlockSpec(shape, lambda *_: (0,) * nd, pipeline_mode=pl.Buffered(1))


def _params(n_axes):
    return pltpu.CompilerParams(dimension_semantics=("parallel",) * n_axes,
                                vmem_limit_bytes=VMEM_LIMIT)


def _ffn_kernel(x_ref, gpre_ref, gpost_ref, wg_ref, wu_ref, wo_ref, o_ref):
    x = x_ref[...]
    xn = _rms(x, gpre_ref[...]).astype(BF16)
    acc = None
    for c in range(FFN_CHUNKS):
        g = jnp.dot(xn, wg_ref[c], preferred_element_type=F32)
        u = jnp.dot(xn, wu_ref[c], preferred_element_type=F32)
        h = (g * jax.nn.sigmoid(g) * u).astype(BF16)
        part = jnp.dot(h, wo_ref[c], preferred_element_type=F32)
        acc = part if acc is None else acc + part
    o_ref[...] = x + 0.5 * _rms(acc, gpost_ref[...])


def _ffn(x, gpre, gpost, wg, wu, wo):
    m = x.shape[0]
    row = pl.BlockSpec((TM, D_MODEL), lambda i: (i, 0))
    return pl.pallas_call(
        _ffn_kernel,
        grid=(m // TM,),
        in_specs=[row, _resident(gpre.shape), _resident(gpost.shape),
                  _resident(wg.shape), _resident(wu.shape), _resident(wo.shape)],
        out_specs=row,
        out_shape=jax.ShapeDtypeStruct(x.shape, x.dtype),
        compiler_params=_params(1),
        name="ffn",
    )(x, gpre, gpost, wg, wu, wo)


def _qkv_kernel(x_ref, g_ref, w_ref, q_ref, k_ref, v_ref):
    u = _rms(x_ref[...], g_ref[...]).astype(BF16)
    p = jnp.dot(u, w_ref[...], preferred_element_type=F32)
    q_ref[...] = p[:, :D_ATT].astype(BF16)
    k_ref[...] = p[:, D_ATT:2 * D_ATT].astype(BF16)
    v_ref[...] = p[:, 2 * D_ATT:].astype(BF16)


def _qkv(x, g, w):
    m = x.shape[0]
    row = pl.BlockSpec((TM, D_MODEL), lambda i: (i, 0))
    out = jax.ShapeDtypeStruct((m, D_ATT), BF16)
    return pl.pallas_call(
        _qkv_kernel,
        grid=(m // TM,),
        in_specs=[row, _resident(g.shape), _resident(w.shape)],
        out_specs=[row, row, row],
        out_shape=[out, out, out],
        compiler_params=_params(1),
        name="qkv",
    )(x, g, w)


def _att_kernel(q_ref, kp_ref, kc_ref, kn_ref, vp_ref, vc_ref, vn_ref, bias_ref, o_ref,
                kbuf, vbuf, *, rows):
    tile_tokens = ATT_ROWS * GRID_W
    kbuf[0:HALO_TOKENS] = kp_ref[...]
    kbuf[HALO_TOKENS:HALO_TOKENS + tile_tokens] = kc_ref[...]
    kbuf[HALO_TOKENS + tile_tokens:] = kn_ref[...]
    vbuf[0:HALO_TOKENS] = vp_ref[...]
    vbuf[HALO_TOKENS:HALO_TOKENS + tile_tokens] = vc_ref[...]
    vbuf[HALO_TOKENS + tile_tokens:] = vn_ref[...]

    r0 = pl.program_id(1) * ATT_ROWS
    scale = NA_HEAD_DIM ** -0.5
    lane_head = lax.broadcasted_iota(jnp.int32, (GRID_W, LANES), 1) // NA_HEAD_DIM
    head_masks = [lane_head == h for h in range(HEADS_PER_GROUP)]

    def row_body(j, carry):
        r = r0 + j
        r_start = jnp.clip(r - NA_KH // 2, 0, rows - NA_KH)
        delta = r_start - r + NA_KH - 1
        woff = pl.multiple_of((r_start - r0 + HALO_ROWS) * GRID_W, GRID_W)
        qoff = pl.multiple_of(j * GRID_W, GRID_W)
        for g in range(N_GROUPS):
            lanes = slice(g * LANES, (g + 1) * LANES)
            qg = q_ref[pl.ds(qoff, GRID_W), lanes]
            qcat = jnp.concatenate([jnp.where(mk, qg, jnp.zeros_like(qg)) for mk in head_masks], axis=0)
            kw = kbuf[pl.ds(woff, WIN_TOKENS), lanes]
            vw = vbuf[pl.ds(woff, WIN_TOKENS), lanes]
            s = lax.dot_general(qcat, kw, (((1,), (1,)), ((), ())), preferred_element_type=F32)
            bias = jnp.concatenate(
                [jnp.concatenate([bias_ref[g * HEADS_PER_GROUP + h, delta + 2 * ip]
                                  for ip in range(NA_KH // 2)], axis=1)
                 for h in range(HEADS_PER_GROUP)], axis=0)
            s = s * scale + bias
            m = jnp.max(s, axis=-1, keepdims=True)
            p = jnp.exp(s - m)
            l = jnp.sum(p, axis=-1, keepdims=True)
            pv = jnp.dot(p.astype(BF16), vw, preferred_element_type=F32) / l
            o = None
            for h, mk in enumerate(head_masks):
                part = jnp.where(mk, pv[h * GRID_W:(h + 1) * GRID_W], 0.0)
                o = part if o is None else o + part
            o_ref[pl.ds(qoff, GRID_W), lanes] = o.astype(BF16)
        return carry

    lax.fori_loop(0, ATT_ROWS, row_body, 0)


def _attention(q, k, v, bias):
    bsz, t, _ = q.shape
    rows = t // GRID_W
    tile_tokens = ATT_ROWS * GRID_W
    n_tiles = rows // ATT_ROWS
    halo_per_tile = ATT_ROWS // HALO_ROWS
    n_halo = rows // HALO_ROWS
    cur = pl.BlockSpec((None, tile_tokens, D_ATT), lambda b, i: (b, i, 0))
    prev = pl.BlockSpec((None, HALO_TOKENS, D_ATT),
                        lambda b, i: (b, jnp.maximum(i * halo_per_tile - 1, 0), 0))
    nxt = pl.BlockSpec((None, HALO_TOKENS, D_ATT),
                       lambda b, i: (b, jnp.minimum((i + 1) * halo_per_tile, n_halo - 1), 0))
    buf = pltpu.VMEM((tile_tokens + 2 * HALO_TOKENS, D_ATT), BF16)
    return pl.pallas_call(
        functools.partial(_att_kernel, rows=rows),
        grid=(bsz, n_tiles),
        in_specs=[cur, prev, cur, nxt, prev, cur, nxt, _resident(bias.shape)],
        out_specs=cur,
        out_shape=jax.ShapeDtypeStruct(q.shape, BF16),
        scratch_shapes=[buf, buf],
        compiler_params=_params(2),
        name="att",
    )(q, k, k, k, v, v, v, bias)


def _bias_table(rpb):
    qc = jnp.arange(GRID_W)
    kc = jnp.arange(GRID_W)
    c_start = jnp.clip(qc - NA_KW // 2, 0, GRID_W - NA_KW)
    col_ok = (kc[None, :] >= c_start[:, None]) & (kc[None, :] < c_start[:, None] + NA_KW)
    dc = jnp.clip(kc[None, :] - qc[:, None], -(NA_KW - 1), NA_KW - 1) + NA_KW - 1
    full = jnp.where(col_ok[None, None], rpb[:, :, dc], NEG_INF)
    return jnp.concatenate([full[:, :-1], full[:, 1:]], axis=-1)


def _mix_kernel(x_ref, xp_ref, xn_ref, att_ref, gpre_ref, gpost_ref, wc_ref, wgate_ref, bgate_ref,
                convw_ref, wcb_ref, wab_ref, wout_ref, o_ref):
    i = pl.program_id(1)
    n = pl.num_programs(1)
    x = x_ref[...]
    gpre = gpre_ref[...]
    u = _rms(x, gpre).astype(BF16)
    pc = jnp.dot(u, wc_ref[...], preferred_element_type=F32)
    cb = pc[:, :D_CONV]
    z = pc[:, D_CONV:2 * D_CONV] * pc[:, 2 * D_CONV:]

    xh = jnp.concatenate([xp_ref[...], xn_ref[...]], axis=0)
    uh = _rms(xh, gpre).astype(BF16)
    ph = jnp.dot(uh, wc_ref[:, D_CONV:], preferred_element_type=F32)
    zh = ph[:, :D_CONV] * ph[:, D_CONV:]
    z_before = jnp.where(i > 0, zh[CONV_HALO - 1:CONV_HALO], 0.0)
    z_after = jnp.where(i < n - 1, zh[CONV_HALO:CONV_HALO + 1], 0.0)

    tok = lax.broadcasted_iota(jnp.int32, (TM, 1), 0)
    z_left = jnp.where(tok == 0, z_before, pltpu.roll(z, 1, 0))
    z_right = jnp.where(tok == TM - 1, z_after, pltpu.roll(z, TM - 1, 0))
    cw = convw_ref[...]
    conv = z_left * cw[0:1] + z * cw[1:2] + z_right * cw[2:3]
    y_conv = jnp.dot((cb * conv).astype(BF16), wcb_ref[...], preferred_element_type=F32)
    y_att = jnp.dot(att_ref[...], wab_ref[...], preferred_element_type=F32)

    gl = jnp.dot(u, wgate_ref[...], preferred_element_type=F32) + bgate_ref[...]
    gates = jax.nn.sigmoid(gl)
    merged = gates[:, :D_MODEL] * y_conv + gates[:, D_MODEL:] * y_att
    mixed = jnp.dot(merged.astype(BF16), wout_ref[...], preferred_element_type=F32)
    o_ref[...] = x + _rms(mixed, gpost_ref[...])


def _mix(x, att, gpre, gpost, wc, wgate, bgate, convw, wcb, wab, wout):
    bsz, t, _ = x.shape
    n_tiles = t // TM
    halo_per_tile = TM // CONV_HALO
    n_halo = t // CONV_HALO
    row = pl.BlockSpec((None, TM, D_MODEL), lambda b, i: (b, i, 0))
    prev = pl.BlockSpec((None, CONV_HALO, D_MODEL),
                        lambda b, i: (b, jnp.maximum(i * halo_per_tile - 1, 0), 0))
    nxt = pl.BlockSpec((None, CONV_HALO, D_MODEL),
                       lambda b, i: (b, jnp.minimum((i + 1) * halo_per_tile, n_halo - 1), 0))
    consts = (gpre, gpost, wc, wgate, bgate, convw, wcb, wab, wout)
    return pl.pallas_call(
        _mix_kernel,
        grid=(bsz, n_tiles),
        in_specs=[row, prev, nxt, row] + [_resident(c.shape) for c in consts],
        out_specs=row,
        out_shape=jax.ShapeDtypeStruct(x.shape, x.dtype),
        compiler_params=_params(2),
        name="mix",
    )(x, x, x, att, *consts)


def _ffn_weights(w_in, w_out):
    wg = w_in[:, :D_FF].reshape(D_MODEL, FFN_CHUNKS, FFN_CHUNK).transpose(1, 0, 2).astype(BF16)
    wu = w_in[:, D_FF:].reshape(D_MODEL, FFN_CHUNKS, FFN_CHUNK).transpose(1, 0, 2).astype(BF16)
    wo = w_out.reshape(FFN_CHUNKS, FFN_CHUNK, D_MODEL).astype(BF16)
    return wg, wu, wo


def _layer(x, p):
    bsz, t, d = x.shape
    flat = lambda a: a.reshape(bsz * t, a.shape[-1])
    x = _ffn(flat(x), p["g1pre"], p["g1post"], *p["ffn1"])
    q, k, v = _qkv(x, p["gmpre"], p["wqkv"])
    seq = lambda a: a.reshape(bsz, t, a.shape[-1])
    att = _attention(seq(q), seq(k), seq(v), p["bias"])
    x = _mix(seq(x), att, p["gmpre"], p["gmpost"], p["wc"], p["wgate"], p["bgate"], p["convw"],
             p["wcb"], p["wab"], p["wout"])
    x = _ffn(flat(x), p["g2pre"], p["g2post"], *p["ffn2"])
    return x.reshape(bsz, t, d)


def kernel(x_prompt, x_sample, g_ffn1_pre, g_ffn1_post, w_ffn1_in, w_ffn1_out, g_mix_pre, g_mix_post, w_mix_in, b_mix_gate, conv_w, na_rpb, w_conv_branch, w_att_branch, w_mix_out, g_ffn2_pre, g_ffn2_post, w_ffn2_in, w_ffn2_out):
    depth = w_mix_in.shape[0]
    vec = lambda a: a.reshape(1, -1)
    y_prompt, y_sample = x_prompt, x_sample
    for l in range(depth):
        wm = w_mix_in[l]
        p = dict(
            g1pre=vec(g_ffn1_pre[l]), g1post=vec(g_ffn1_post[l]),
            ffn1=_ffn_weights(w_ffn1_in[l], w_ffn1_out[l]),
            gmpre=vec(g_mix_pre[l]), gmpost=vec(g_mix_post[l]),
            wc=wm[:, :3 * D_CONV].astype(BF16),
            wqkv=wm[:, 3 * D_CONV:3 * D_CONV + 3 * D_ATT].astype(BF16),
            wgate=wm[:, 3 * D_CONV + 3 * D_ATT:].astype(BF16),
            bgate=vec(b_mix_gate[l]), convw=conv_w[l],
            bias=_bias_table(na_rpb[l]),
            wcb=w_conv_branch[l].astype(BF16), wab=w_att_branch[l].astype(BF16),
            wout=w_mix_out[l].astype(BF16),
            g2pre=vec(g_ffn2_pre[l]), g2post=vec(g_ffn2_post[l]),
            ffn2=_ffn_weights(w_ffn2_in[l], w_ffn2_out[l]),
        )
        y_prompt = _layer(y_prompt, p)
        y_sample = _layer(y_sample, p)
    return (y_prompt, y_sample)
```

```python
import functools

import jax
import jax.numpy as jnp
from jax import lax
from jax.experimental import pallas as pl
from jax.experimental.pallas import tpu as pltpu

D_MODEL = 1024
GRID_W = 64
D_CONV = 1024
CONV_W = 3
NA_HEADS = 32
NA_HEAD_DIM = 32
D_ATT = NA_HEADS * NA_HEAD_DIM
NA_KH = 8
NA_KW = 16
D_FF = 2816
RMS_EPS = 1e-6
NEG_INF = -1e30

LANES = 128
HEADS_PER_GROUP = LANES // NA_HEAD_DIM
N_GROUPS = NA_HEADS // HEADS_PER_GROUP
HALO_ROWS = 4
CONV_HALO = 8

QUARTERS = GRID_W // NA_KW
QCOLS = GRID_W // QUARTERS
assert NA_KH * QCOLS == LANES
SUBLANES = 8
PAIR = 16
LOG2E = 1.4426950408889634
Q_SCALE = NA_HEAD_DIM ** -0.5 * LOG2E


def _quarters_seen(q_lo, q_hi):
    seen = set()
    for qc in range(q_lo, q_hi):
        c0 = min(max(qc - NA_KW // 2, 0), GRID_W - NA_KW)
        seen |= {c // QCOLS for c in range(c0, c0 + NA_KW)}
    return list(range(min(seen), max(seen) + 1))


BIAS_TILES = [(j, a) for j in range(GRID_W // SUBLANES)
              for a in _quarters_seen(j * SUBLANES, (j + 1) * SUBLANES)]
BIAS_TILE_INDEX = {ja: n for n, ja in enumerate(BIAS_TILES)}

FFN_CHUNKS = 2
FFN_CHUNK = D_FF // FFN_CHUNKS
TM = 512
ATT_ROWS = 16
GROUPS_IN_FLIGHT = 4
VMEM_LIMIT = 56 * 1024 * 1024

F32 = jnp.float32
BF16 = jnp.bfloat16


def _rms(xf, g):
    y = xf * lax.rsqrt(jnp.mean(xf * xf, axis=-1, keepdims=True) + RMS_EPS)
    return y * g


def _resident(shape):
    nd = len(shape)
    return pl.BlockSpec(shape, lambda *_: (0,) * nd, pipeline_mode=pl.Buffered(1))


def _params(n_axes, flags=None):
    return pltpu.CompilerParams(dimension_semantics=("parallel",) * n_axes,
                                vmem_limit_bytes=VMEM_LIMIT, flags=flags)


def _ffn_kernel(x_ref, gpre_ref, gpost_ref, wg_ref, wu_ref, wo_ref, o_ref):
    x = x_ref[...]
    xn = _rms(x, gpre_ref[...]).astype(BF16)
    acc = None
    for c in range(FFN_CHUNKS):
        g = jnp.dot(xn, wg_ref[c], preferred_element_type=F32)
        u = jnp.dot(xn, wu_ref[c], preferred_element_type=F32)
        h = (g * jax.nn.sigmoid(g) * u).astype(BF16)
        part = jnp.dot(h, wo_ref[c], preferred_element_type=F32)
        acc = part if acc is None else acc + part
    o_ref[...] = x + 0.5 * _rms(acc, gpost_ref[...])


def _ffn(x, gpre, gpost, wg, wu, wo):
    m = x.shape[0]
    row = pl.BlockSpec((TM, D_MODEL), lambda i: (i, 0))
    return pl.pallas_call(
        _ffn_kernel,
        grid=(m // TM,),
        in_specs=[row, _resident(gpre.shape), _resident(gpost.shape),
                  _resident(wg.shape), _resident(wu.shape), _resident(wo.shape)],
        out_specs=row,
        out_shape=jax.ShapeDtypeStruct(x.shape, x.dtype),
        compiler_params=_params(1),
        name="ffn",
    )(x, gpre, gpost, wg, wu, wo)


def _qkv_kernel(x_ref, g_ref, w_ref, q_ref, k_ref, v_ref):
    u = _rms(x_ref[...], g_ref[...]).astype(BF16)
    p = jnp.dot(u, w_ref[...], preferred_element_type=F32)
    q_ref[...] = (p[:, :D_ATT] * Q_SCALE).astype(BF16)
    for r in range(TM // GRID_W):
        for a in range(QUARTERS):
            src = slice(r * GRID_W + a * QCOLS, r * GRID_W + (a + 1) * QCOLS)
            dst = slice(r * QCOLS, (r + 1) * QCOLS)
            k_ref[a, dst, :] = p[src, D_ATT:2 * D_ATT].astype(BF16)
            v_ref[a, dst, :] = p[src, 2 * D_ATT:].astype(BF16)


def _qkv(x, g, w):
    bsz, t, _ = x.shape
    row = pl.BlockSpec((None, TM, D_MODEL), lambda b, i: (b, i, 0))
    quartered = pl.BlockSpec((None, QUARTERS, TM // QUARTERS, D_ATT), lambda b, i: (b, 0, i, 0))
    kv = jax.ShapeDtypeStruct((bsz, QUARTERS, t // QUARTERS, D_ATT), BF16)
    return pl.pallas_call(
        _qkv_kernel,
        grid=(bsz, t // TM),
        in_specs=[row, _resident(g.shape), _resident(w.shape)],
        out_specs=[row, quartered, quartered],
        out_shape=[jax.ShapeDtypeStruct((bsz, t, D_ATT), BF16), kv, kv],
        compiler_params=_params(2),
        name="qkv",
    )(x, g, w)


def _att_kernel(q_ref, kp_ref, kc_ref, kn_ref, vp_ref, vc_ref, vn_ref, bias_ref, o_ref,
                kbuf, vbuf, *, rows):
    halo = HALO_ROWS * QCOLS
    tile = ATT_ROWS * QCOLS
    kbuf[:, 0:halo] = kp_ref[...]
    kbuf[:, halo:halo + tile] = kc_ref[...]
    kbuf[:, halo + tile:] = kn_ref[...]
    vbuf[:, 0:halo] = vp_ref[...]
    vbuf[:, halo:halo + tile] = vc_ref[...]
    vbuf[:, halo + tile:] = vn_ref[...]

    r0 = pl.program_id(1) * ATT_ROWS
    lane_head = lax.broadcasted_iota(jnp.int32, (GRID_W, LANES), 1) // NA_HEAD_DIM
    head_masks = [lane_head == h for h in range(HEADS_PER_GROUP)]
    masked_tile = jnp.full((SUBLANES, LANES), NEG_INF, F32)

    def row_body(j, carry):
        r = r0 + j
        r_start = jnp.clip(r - NA_KH // 2, 0, rows - NA_KH)
        cfg = r - r_start
        woff = pl.multiple_of((r_start - r0 + HALO_ROWS) * QCOLS, QCOLS)
        qoff = pl.multiple_of(j * GRID_W, GRID_W)
        def window(buf, g):
            return jnp.concatenate([buf[a, pl.ds(woff, LANES), g * LANES:(g + 1) * LANES]
                                    for a in range(QUARTERS)], axis=0)

        def scores(g):
            qg = q_ref[pl.ds(qoff, GRID_W), g * LANES:(g + 1) * LANES]
            qcat = jnp.concatenate([jnp.where(mk, qg, jnp.zeros_like(qg)) for mk in head_masks], axis=0)
            return lax.dot_general(qcat, window(kbuf, g), (((1,), (1,)), ((), ())),
                                   preferred_element_type=F32)

        def softmax(g, s):
            p_rows = []
            for h in range(HEADS_PER_GROUP):
                head = g * HEADS_PER_GROUP + h
                for k in range(GRID_W // PAIR):
                    quarters = _quarters_seen(k * PAIR, (k + 1) * PAIR)
                    lo, hi = quarters[0], quarters[-1] + 1
                    bias = jnp.concatenate(
                        [jnp.concatenate(
                            [bias_ref[cfg, head, BIAS_TILE_INDEX[(jj, a)]]
                             if (jj, a) in BIAS_TILE_INDEX else masked_tile for a in quarters], axis=1)
                         for jj in range(k * PAIR // SUBLANES, (k + 1) * PAIR // SUBLANES)], axis=0)
                    q_lo = h * GRID_W + k * PAIR
                    x = s[q_lo:q_lo + PAIR, lo * LANES:hi * LANES] + bias
                    m = jnp.max(x, axis=-1, keepdims=True)
                    parts = [jnp.exp2(x - m).astype(BF16)]
                    if lo > 0:
                        parts.insert(0, jnp.zeros((PAIR, lo * LANES), BF16))
                    if hi < QUARTERS:
                        parts.append(jnp.zeros((PAIR, (QUARTERS - hi) * LANES), BF16))
                    p_rows.append(jnp.concatenate(parts, axis=1))
            return jnp.concatenate(p_rows, axis=0)

        def output(g, probs):
            vw = jnp.concatenate([window(vbuf, g), jnp.ones((QUARTERS * LANES, LANES), BF16)], axis=1)
            pv = jnp.dot(probs, vw, preferred_element_type=F32)
            pv = pv[:, :LANES] / pv[:, LANES:]
            o = None
            for h, mk in enumerate(head_masks):
                part = jnp.where(mk, pv[h * GRID_W:(h + 1) * GRID_W], 0.0)
                o = part if o is None else o + part
            o_ref[pl.ds(qoff, GRID_W), g * LANES:(g + 1) * LANES] = o.astype(BF16)

        for g0 in range(0, N_GROUPS, GROUPS_IN_FLIGHT):
            gs = range(g0, g0 + GROUPS_IN_FLIGHT)
            s = [scores(g) for g in gs]
            probs = [softmax(g, sg) for g, sg in zip(gs, s)]
            for g, pg in zip(gs, probs):
                output(g, pg)
        return carry

    lax.fori_loop(0, ATT_ROWS, row_body, 0)


def _attention(q, k, v, bias):
    bsz, t, _ = q.shape
    rows = t // GRID_W
    halo_per_tile = ATT_ROWS // HALO_ROWS
    n_halo = rows // HALO_ROWS
    cur_q = pl.BlockSpec((None, ATT_ROWS * GRID_W, D_ATT), lambda b, i: (b, i, 0))
    cur = pl.BlockSpec((None, QUARTERS, ATT_ROWS * QCOLS, D_ATT), lambda b, i: (b, 0, i, 0))
    prev = pl.BlockSpec((None, QUARTERS, HALO_ROWS * QCOLS, D_ATT),
                        lambda b, i: (b, 0, jnp.maximum(i * halo_per_tile - 1, 0), 0))
    nxt = pl.BlockSpec((None, QUARTERS, HALO_ROWS * QCOLS, D_ATT),
                       lambda b, i: (b, 0, jnp.minimum((i + 1) * halo_per_tile, n_halo - 1), 0))
    buf = pltpu.VMEM((QUARTERS, (ATT_ROWS + 2 * HALO_ROWS) * QCOLS, D_ATT), BF16)
    return pl.pallas_call(
        functools.partial(_att_kernel, rows=rows),
        grid=(bsz, rows // ATT_ROWS),
        in_specs=[cur_q, prev, cur, nxt, prev, cur, nxt, _resident(bias.shape)],
        out_specs=cur_q,
        out_shape=jax.ShapeDtypeStruct(q.shape, BF16),
        scratch_shapes=[buf, buf],
        compiler_params=_params(2),
        name="att",
    )(q, k, k, k, v, v, v, bias)


def _bias_table(rpb):
    qc = jnp.arange(GRID_W)
    kc = jnp.arange(GRID_W)
    c_start = jnp.clip(qc - NA_KW // 2, 0, GRID_W - NA_KW)
    col_ok = (kc[None, :] >= c_start[:, None]) & (kc[None, :] < c_start[:, None] + NA_KW)
    dc = jnp.clip(kc[None, :] - qc[:, None], -(NA_KW - 1), NA_KW - 1) + NA_KW - 1
    full = jnp.where(col_ok[None, None], rpb[:, :, dc] * LOG2E, NEG_INF)
    tiles_j = jnp.array([j for j, _ in BIAS_TILES])
    tiles_a = jnp.array([a for _, a in BIAS_TILES])
    per_cfg = []
    for c in range(NA_KH):
        win = full[:, NA_KH - 1 - c:2 * NA_KH - 1 - c]
        win = win.reshape(NA_HEADS, NA_KH, GRID_W // SUBLANES, SUBLANES, QUARTERS, QCOLS)
        win = win.transpose(0, 2, 4, 3, 1, 5)
        win = win.reshape(NA_HEADS, GRID_W // SUBLANES, QUARTERS, SUBLANES, LANES)
        per_cfg.append(win[:, tiles_j, tiles_a])
    return jnp.stack(per_cfg)


def _mix_kernel(x_ref, xp_ref, xn_ref, att_ref, gpre_ref, gpost_ref, wc_ref, wgate_ref, bgate_ref,
                convw_ref, wcb_ref, wab_ref, wout_ref, o_ref):
    i = pl.program_id(1)
    n = pl.num_programs(1)
    x = x_ref[...]
    gpre = gpre_ref[...]
    u = _rms(x, gpre).astype(BF16)
    pc = jnp.dot(u, wc_ref[...], preferred_element_type=F32)
    cb = pc[:, :D_CONV]
    z = pc[:, D_CONV:2 * D_CONV] * pc[:, 2 * D_CONV:]

    xh = jnp.concatenate([xp_ref[...], xn_ref[...]], axis=0)
    uh = _rms(xh, gpre).astype(BF16)
    ph = jnp.dot(uh, wc_ref[:, D_CONV:], preferred_element_type=F32)
    zh = ph[:, :D_CONV] * ph[:, D_CONV:]
    z_before = jnp.where(i > 0, zh[CONV_HALO - 1:CONV_HALO], 0.0)
    z_after = jnp.where(i < n - 1, zh[CONV_HALO:CONV_HALO + 1], 0.0)

    tok = lax.broadcasted_iota(jnp.int32, (TM, 1), 0)
    z_left = jnp.where(tok == 0, z_before, pltpu.roll(z, 1, 0))
    z_right = jnp.where(tok == TM - 1, z_after, pltpu.roll(z, TM - 1, 0))
    cw = convw_ref[...]
    conv = z_left * cw[0:1] + z * cw[1:2] + z_right * cw[2:3]
    y_conv = jnp.dot((cb * conv).astype(BF16), wcb_ref[...], preferred_element_type=F32)
    y_att = jnp.dot(att_ref[...], wab_ref[...], preferred_element_type=F32)

    gl = jnp.dot(u, wgate_ref[...], preferred_element_type=F32) + bgate_ref[...]
    gates = jax.nn.sigmoid(gl)
    merged = gates[:, :D_MODEL] * y_conv + gates[:, D_MODEL:] * y_att
    mixed = jnp.dot(merged.astype(BF16), wout_ref[...], preferred_element_type=F32)
    o_ref[...] = x + _rms(mixed, gpost_ref[...])


def _mix(x, att, gpre, gpost, wc, wgate, bgate, convw, wcb, wab, wout):
    bsz, t, _ = x.shape
    n_tiles = t // TM
    halo_per_tile = TM // CONV_HALO
    n_halo = t // CONV_HALO
    row = pl.BlockSpec((None, TM, D_MODEL), lambda b, i: (b, i, 0))
    prev = pl.BlockSpec((None, CONV_HALO, D_MODEL),
                        lambda b, i: (b, jnp.maximum(i * halo_per_tile - 1, 0), 0))
    nxt = pl.BlockSpec((None, CONV_HALO, D_MODEL),
                       lambda b, i: (b, jnp.minimum((i + 1) * halo_per_tile, n_halo - 1), 0))
    consts = (gpre, gpost, wc, wgate, bgate, convw, wcb, wab, wout)
    return pl.pallas_call(
        _mix_kernel,
        grid=(bsz, n_tiles),
        in_specs=[row, prev, nxt, row] + [_resident(c.shape) for c in consts],
        out_specs=row,
        out_shape=jax.ShapeDtypeStruct(x.shape, x.dtype),
        compiler_params=_params(2),
        name="mix",
    )(x, x, x, att, *consts)


def _ffn_weights(w_in, w_out):
    wg = w_in[:, :D_FF].reshape(D_MODEL, FFN_CHUNKS, FFN_CHUNK).transpose(1, 0, 2).astype(BF16)
    wu = w_in[:, D_FF:].reshape(D_MODEL, FFN_CHUNKS, FFN_CHUNK).transpose(1, 0, 2).astype(BF16)
    wo = w_out.reshape(FFN_CHUNKS, FFN_CHUNK, D_MODEL).astype(BF16)
    return wg, wu, wo


def _layer(x, p):
    bsz, t, d = x.shape
    flat = lambda a: a.reshape(bsz * t, a.shape[-1])
    x = _ffn(flat(x), p["g1pre"], p["g1post"], *p["ffn1"]).reshape(bsz, t, d)
    q, k, v = _qkv(x, p["gmpre"], p["wqkv"])
    att = _attention(q, k, v, p["bias"])
    x = _mix(x, att, p["gmpre"], p["gmpost"], p["wc"], p["wgate"], p["bgate"], p["convw"],
             p["wcb"], p["wab"], p["wout"])
    x = _ffn(flat(x), p["g2pre"], p["g2post"], *p["ffn2"])
    return x.reshape(bsz, t, d)


def kernel(x_prompt, x_sample, g_ffn1_pre, g_ffn1_post, w_ffn1_in, w_ffn1_out, g_mix_pre, g_mix_post, w_mix_in, b_mix_gate, conv_w, na_rpb, w_conv_branch, w_att_branch, w_mix_out, g_ffn2_pre, g_ffn2_post, w_ffn2_in, w_ffn2_out):
    depth = w_mix_in.shape[0]
    vec = lambda a: a.reshape(1, -1)
    y_prompt, y_sample = x_prompt, x_sample
    for l in range(depth):
        wm = w_mix_in[l]
        p = dict(
            g1pre=vec(g_ffn1_pre[l]), g1post=vec(g_ffn1_post[l]),
            ffn1=_ffn_weights(w_ffn1_in[l], w_ffn1_out[l]),
            gmpre=vec(g_mix_pre[l]), gmpost=vec(g_mix_post[l]),
            wc=wm[:, :3 * D_CONV].astype(BF16),
            wqkv=wm[:, 3 * D_CONV:3 * D_CONV + 3 * D_ATT].astype(BF16),
            wgate=wm[:, 3 * D_CONV + 3 * D_ATT:].astype(BF16),
            bgate=vec(b_mix_gate[l]), convw=conv_w[l],
            bias=_bias_table(na_rpb[l]),
            wcb=w_conv_branch[l].astype(BF16), wab=w_att_branch[l].astype(BF16),
            wout=w_mix_out[l].astype(BF16),
            g2pre=vec(g_ffn2_pre[l]), g2post=vec(g_ffn2_post[l]),
            ffn2=_ffn_weights(w_ffn2_in[l], w_ffn2_out[l]),
        )
        y_prompt = _layer(y_prompt, p)
        y_sample = _layer(y_sample, p)
    return (y_prompt, y_sample)
```

```python
import functools

import jax
import jax.numpy as jnp
import numpy as np
from jax import lax
from jax.experimental import pallas as pl
from jax.experimental.pallas import tpu as pltpu

D_MODEL = 1024
GRID_W = 64
D_CONV = 1024
CONV_W = 3
NA_HEADS = 32
NA_HEAD_DIM = 32
D_ATT = NA_HEADS * NA_HEAD_DIM
NA_KH = 8
NA_KW = 16
D_FF = 2816
RMS_EPS = 1e-6
NEG_INF = -1e30

LANES = 128
HEADS_PER_GROUP = LANES // NA_HEAD_DIM
N_GROUPS = NA_HEADS // HEADS_PER_GROUP
HALO_ROWS = 4
CONV_HALO = 8

QUARTERS = GRID_W // NA_KW
QCOLS = GRID_W // QUARTERS
assert NA_KH * QCOLS == LANES
MXU_WIDTH = 256
QTILE = MXU_WIDTH // LANES
SUBLANES = 8
PAIR = 16
LOG2E = 1.4426950408889634
Q_SCALE = NA_HEAD_DIM ** -0.5 * LOG2E


def _quarters_seen(q_lo, q_hi):
    seen = set()
    for qc in range(q_lo, q_hi):
        c0 = min(max(qc - NA_KW // 2, 0), GRID_W - NA_KW)
        seen |= {c // QCOLS for c in range(c0, c0 + NA_KW)}
    return list(range(min(seen), max(seen) + 1))


BIAS_TILES = [(j, a) for j in range(GRID_W // SUBLANES)
              for a in _quarters_seen(j * SUBLANES, (j + 1) * SUBLANES)]
BIAS_TILE_INDEX = {ja: n for n, ja in enumerate(BIAS_TILES)}

FFN_CHUNKS = 2
FFN_CHUNK = D_FF // FFN_CHUNKS
TM = 512
FFN_TM = 1024
FFN_SUB = 512
QKV_TM = 1024
QKV_SUB = 512
ATT_ROWS = 16
GROUPS_IN_FLIGHT = 8
ROWS_PER_ITER = 4
VMEM_LIMIT = 56 * 1024 * 1024

F32 = jnp.float32
BF16 = jnp.bfloat16


def _rms(xf, g):
    y = xf * lax.rsqrt(jnp.mean(xf * xf, axis=-1, keepdims=True) + RMS_EPS)
    return y * g


def _resident(shape):
    nd = len(shape)
    return pl.BlockSpec(shape, lambda *_: (0,) * nd, pipeline_mode=pl.Buffered(1))


def _params(n_axes, flags=None):
    return pltpu.CompilerParams(dimension_semantics=("parallel",) * n_axes,
                                vmem_limit_bytes=VMEM_LIMIT, flags=flags)


def _ffn_kernel(x_ref, gpre_ref, gpost_ref, win_ref, wout_ref, o_ref):
    subs = [slice(i * FFN_SUB, (i + 1) * FFN_SUB) for i in range(FFN_TM // FFN_SUB)]
    xn = [None] * len(subs)
    acc = [None] * len(subs)
    for c in range(FFN_CHUNKS):
        cols = slice(c * FFN_CHUNK, (c + 1) * FFN_CHUNK)
        up_cols = slice(D_FF + c * FFN_CHUNK, D_FF + (c + 1) * FFN_CHUNK)
        for i, s in enumerate(subs):
            if c == 0:
                xn[i] = _rms(x_ref[s, :], gpre_ref[...]).astype(BF16)
            g = jnp.dot(xn[i], win_ref[:, cols], preferred_element_type=F32)
            u = jnp.dot(xn[i], win_ref[:, up_cols], preferred_element_type=F32)
            h = (g * jax.nn.sigmoid(g) * u).astype(BF16)
            part = jnp.dot(h, wout_ref[cols, :], preferred_element_type=F32)
            acc[i] = part if acc[i] is None else acc[i] + part
    for i, s in enumerate(subs):
        o_ref[s, :] = x_ref[s, :] + 0.5 * _rms(acc[i], gpost_ref[...])


def _ffn(x, gpre, gpost, w_in, w_out):
    m = x.shape[0]
    row = pl.BlockSpec((FFN_TM, D_MODEL), lambda i: (i, 0))
    return pl.pallas_call(
        _ffn_kernel,
        grid=(m // FFN_TM,),
        in_specs=[row, _resident(gpre.shape), _resident(gpost.shape),
                  _resident(w_in.shape), _resident(w_out.shape)],
        out_specs=row,
        out_shape=jax.ShapeDtypeStruct(x.shape, x.dtype),
        compiler_params=_params(1),
        name="ffn",
    )(x, gpre, gpost, w_in, w_out)


def _qkv_kernel(x_ref, g_ref, w_ref, q_ref, k_ref, v_ref):
    n_sub = QKV_TM // QKV_SUB
    ps = [jnp.dot(_rms(x_ref[i * QKV_SUB:(i + 1) * QKV_SUB, :], g_ref[...]).astype(BF16), w_ref[...],
                  preferred_element_type=F32) for i in range(n_sub)]
    for i, p in enumerate(ps):
        q_ref[i * QKV_SUB:(i + 1) * QKV_SUB, :] = (p[:, :D_ATT] * Q_SCALE).astype(BF16)
        for r in range(QKV_SUB // GRID_W):
            for a in range(QUARTERS):
                src = slice(r * GRID_W + a * QCOLS, r * GRID_W + (a + 1) * QCOLS)
                row0 = (i * QKV_SUB // GRID_W + r) * QCOLS
                k_ref[a, row0:row0 + QCOLS, :] = p[src, D_ATT:2 * D_ATT].astype(BF16)
                v_ref[a, row0:row0 + QCOLS, :] = p[src, 2 * D_ATT:].astype(BF16)


def _qkv(x, g, w):
    bsz, t, _ = x.shape
    row = pl.BlockSpec((None, QKV_TM, D_MODEL), lambda b, i: (b, i, 0))
    quartered = pl.BlockSpec((None, QUARTERS, QKV_TM // QUARTERS, D_ATT), lambda b, i: (b, 0, i, 0))
    kv = jax.ShapeDtypeStruct((bsz, QUARTERS, t // QUARTERS, D_ATT), BF16)
    return pl.pallas_call(
        _qkv_kernel,
        grid=(bsz, t // QKV_TM),
        in_specs=[row, _resident(g.shape), _resident(w.shape)],
        out_specs=[row, quartered, quartered],
        out_shape=[jax.ShapeDtypeStruct((bsz, t, D_ATT), BF16), kv, kv],
        compiler_params=_params(2),
        name="qkv",
    )(x, g, w)


def _att_kernel(q_ref, kp_ref, kc_ref, kn_ref, vp_ref, vc_ref, vn_ref, bias_ref, o_ref,
                kbuf, vbuf, *, rows):
    halo = HALO_ROWS * QCOLS
    tile = ATT_ROWS * QCOLS
    kbuf[:, 0:halo] = kp_ref[...]
    kbuf[:, halo:halo + tile] = kc_ref[...]
    kbuf[:, halo + tile:] = kn_ref[...]
    vbuf[:, 0:halo] = vp_ref[...]
    vbuf[:, halo:halo + tile] = vc_ref[...]
    vbuf[:, halo + tile:] = vn_ref[...]

    r0 = pl.program_id(1) * ATT_ROWS
    lane_head = lax.broadcasted_iota(jnp.int32, (PAIR, LANES), 1) // NA_HEAD_DIM
    head_masks = [lane_head == h for h in range(HEADS_PER_GROUP)]
    masked_tile = jnp.full((SUBLANES, LANES), NEG_INF, F32)
    n_pairs = GRID_W // PAIR
    pair_quarters = [_quarters_seen(k * PAIR, (k + 1) * PAIR) for k in range(n_pairs)]
    tile_quarters = [list(range(t * QTILE, (t + 1) * QTILE)) for t in range(QUARTERS // QTILE)]
    tile_pairs = [[k for k in range(n_pairs) if set(pair_quarters[k]) & set(tq)] for tq in tile_quarters]

    def row_stages(j):
        r = r0 + j
        r_start = jnp.clip(r - NA_KH // 2, 0, rows - NA_KH)
        cfg = r - r_start
        woff = pl.multiple_of((r_start - r0 + HALO_ROWS) * QCOLS, QCOLS)
        qoff = pl.multiple_of(j * GRID_W, GRID_W)

        def window(buf, g, quarters):
            return jnp.concatenate([buf[a, pl.ds(woff, LANES), g * LANES:(g + 1) * LANES]
                                    for a in quarters], axis=0)

        def scores(g):
            out = []
            for tq, pairs in zip(tile_quarters, tile_pairs):
                q_lo, n_q = pairs[0] * PAIR, len(pairs) * PAIR
                qg = q_ref[pl.ds(pl.multiple_of(qoff + q_lo, PAIR), n_q), g * LANES:(g + 1) * LANES]
                q_head = lax.broadcasted_iota(jnp.int32, (n_q, LANES), 1) // NA_HEAD_DIM
                qcat = jnp.concatenate(
                    [jnp.where(q_head == h, qg, jnp.zeros_like(qg)) for h in range(HEADS_PER_GROUP)],
                    axis=0)
                out.append(lax.dot_general(qcat, window(kbuf, g, tq), (((1,), (1,)), ((), ())),
                                           preferred_element_type=F32))
            return out

        def softmax(g, s):
            p_rows = [[] for _ in tile_quarters]
            for h in range(HEADS_PER_GROUP):
                head = g * HEADS_PER_GROUP + h
                for k in range(n_pairs):
                    quarters = pair_quarters[k]
                    bias = jnp.concatenate(
                        [jnp.concatenate(
                            [bias_ref[head, cfg, BIAS_TILE_INDEX[(jj, a)]]
                             if (jj, a) in BIAS_TILE_INDEX else masked_tile for a in quarters], axis=1)
                         for jj in range(k * PAIR // SUBLANES, (k + 1) * PAIR // SUBLANES)], axis=0)
                    pieces = []
                    for t, (tq, pairs) in enumerate(zip(tile_quarters, tile_pairs)):
                        seen = [a for a in quarters if a in tq]
                        if seen:
                            row = (h * len(pairs) + k - pairs[0]) * PAIR
                            pieces.append(s[t][row:row + PAIR,
                                               (seen[0] - tq[0]) * LANES:(seen[-1] + 1 - tq[0]) * LANES])
                    x = jnp.concatenate(pieces, axis=1) + bias
                    m = jnp.max(x, axis=-1, keepdims=True)
                    p = jnp.exp2(x - m).astype(BF16)
                    for t, (tq, pairs) in enumerate(zip(tile_quarters, tile_pairs)):
                        if k not in pairs:
                            continue
                        seen = [a for a in quarters if a in tq]
                        parts = [p[:, (seen[0] - quarters[0]) * LANES:(seen[-1] + 1 - quarters[0]) * LANES]]
                        if seen[0] > tq[0]:
                            parts.insert(0, jnp.zeros((PAIR, (seen[0] - tq[0]) * LANES), BF16))
                        if seen[-1] < tq[-1]:
                            parts.append(jnp.zeros((PAIR, (tq[-1] - seen[-1]) * LANES), BF16))
                        p_rows[t].append(jnp.concatenate(parts, axis=1))
            return [jnp.concatenate(rows_t, axis=0) for rows_t in p_rows]

        def output(g, probs):
            ones = jnp.ones((QTILE * LANES, LANES), BF16)
            pv = [jnp.dot(pt, jnp.concatenate([window(vbuf, g, tq), ones], axis=1),
                          preferred_element_type=F32)
                  for pt, tq in zip(probs, tile_quarters)]
            blocks = []
            for k in range(n_pairs):
                o = None
                for h, mk in enumerate(head_masks):
                    acc = None
                    for t, pairs in enumerate(tile_pairs):
                        if k in pairs:
                            row = (h * len(pairs) + k - pairs[0]) * PAIR
                            part = pv[t][row:row + PAIR]
                            acc = part if acc is None else acc + part
                    hk = jnp.where(mk, acc[:, :LANES] / acc[:, LANES:], 0.0)
                    o = hk if o is None else o + hk
                blocks.append(o)
            o_ref[pl.ds(qoff, GRID_W), g * LANES:(g + 1) * LANES] = jnp.concatenate(blocks, axis=0).astype(BF16)

        return scores, softmax, output

    def body(it, carry):
        stages = [row_stages(it * ROWS_PER_ITER + i) for i in range(ROWS_PER_ITER)]
        pending = None
        for i in range(ROWS_PER_ITER):
            scores, softmax, _ = stages[i]
            for g0 in range(0, N_GROUPS, GROUPS_IN_FLIGHT):
                gs = range(g0, g0 + GROUPS_IN_FLIGHT)
                s = [scores(g) for g in gs]
                if pending is not None:
                    for out_fn, g, pg in pending:
                        out_fn(g, pg)
                pending = [(stages[i][2], g, softmax(g, sg)) for g, sg in zip(gs, s)]
        for out_fn, g, pg in pending:
            out_fn(g, pg)
        return carry

    lax.fori_loop(0, ATT_ROWS // ROWS_PER_ITER, body, 0)


def _attention(q, k, v, bias):
    bsz, t, _ = q.shape
    rows = t // GRID_W
    halo_per_tile = ATT_ROWS // HALO_ROWS
    n_halo = rows // HALO_ROWS
    cur_q = pl.BlockSpec((None, ATT_ROWS * GRID_W, D_ATT), lambda b, i: (b, i, 0))
    cur = pl.BlockSpec((None, QUARTERS, ATT_ROWS * QCOLS, D_ATT), lambda b, i: (b, 0, i, 0))
    prev = pl.BlockSpec((None, QUARTERS, HALO_ROWS * QCOLS, D_ATT),
                        lambda b, i: (b, 0, jnp.maximum(i * halo_per_tile - 1, 0), 0))
    nxt = pl.BlockSpec((None, QUARTERS, HALO_ROWS * QCOLS, D_ATT),
                       lambda b, i: (b, 0, jnp.minimum((i + 1) * halo_per_tile, n_halo - 1), 0))
    buf = pltpu.VMEM((QUARTERS, (ATT_ROWS + 2 * HALO_ROWS) * QCOLS, D_ATT), BF16)
    return pl.pallas_call(
        functools.partial(_att_kernel, rows=rows),
        grid=(bsz, rows // ATT_ROWS),
        in_specs=[cur_q, prev, cur, nxt, prev, cur, nxt, _resident(bias.shape)],
        out_specs=cur_q,
        out_shape=jax.ShapeDtypeStruct(q.shape, BF16),
        scratch_shapes=[buf, buf],
        compiler_params=_params(2),
        name="att",
    )(q, k, k, k, v, v, v, bias)


def _bias_index():
    n_dc = 2 * NA_KW - 1
    masked_slot = (2 * NA_KH - 1) * n_dc
    c = np.arange(NA_KH).reshape(-1, 1, 1, 1, 1)
    j = np.array([t[0] for t in BIAS_TILES]).reshape(1, -1, 1, 1, 1)
    a = np.array([t[1] for t in BIAS_TILES]).reshape(1, -1, 1, 1, 1)
    qc = j * SUBLANES + np.arange(SUBLANES).reshape(1, 1, -1, 1, 1)
    i = np.arange(NA_KH).reshape(1, 1, 1, -1, 1)
    kc = a * QCOLS + np.arange(QCOLS).reshape(1, 1, 1, 1, -1)
    c_start = np.clip(qc - NA_KW // 2, 0, GRID_W - NA_KW)
    col_ok = (kc >= c_start) & (kc < c_start + NA_KW)
    dr = NA_KH - 1 - c + i
    dc = np.clip(kc - qc, -(NA_KW - 1), NA_KW - 1) + NA_KW - 1
    idx = np.where(col_ok, dr * n_dc + dc, masked_slot)
    return idx.reshape(NA_KH, len(BIAS_TILES), SUBLANES, LANES).astype(np.int32)


def _bias_table(rpb):
    idx = _bias_index()
    flat = jnp.concatenate([rpb.reshape(NA_HEADS, -1) * LOG2E,
                            jnp.full((NA_HEADS, 1), NEG_INF, F32)], axis=1)
    return jnp.take(flat, idx.reshape(-1), axis=1).reshape((NA_HEADS,) + idx.shape)


def _mix_kernel(x_ref, xp_ref, xn_ref, att_ref, gpre_ref, gpost_ref, wc_ref, wgate_ref, bgate_ref,
                convw_ref, wcb_ref, wab_ref, wout_ref, o_ref):
    i = pl.program_id(1)
    n = pl.num_programs(1)
    x = x_ref[...]
    gpre = gpre_ref[...]
    u_ext = _rms(jnp.concatenate([x, xp_ref[...], xn_ref[...]], axis=0), gpre).astype(BF16)
    u = u_ext[:TM]
    pcv = jnp.dot(u_ext, wc_ref[:, D_CONV:], preferred_element_type=F32)
    z_ext = pcv[:, :D_CONV] * pcv[:, D_CONV:]
    z = z_ext[:TM]
    z_before = jnp.where(i > 0, z_ext[TM + CONV_HALO - 1:TM + CONV_HALO], 0.0)
    z_after = jnp.where(i < n - 1, z_ext[TM + CONV_HALO:TM + CONV_HALO + 1], 0.0)

    y_att = jnp.dot(att_ref[...], wab_ref[...], preferred_element_type=F32)
    gl = jnp.dot(u, wgate_ref[...], preferred_element_type=F32) + bgate_ref[...]
    gates = jax.nn.sigmoid(gl)
    gated_att = gates[:, D_MODEL:] * y_att
    cb = jnp.dot(u, wc_ref[:, :D_CONV], preferred_element_type=F32)

    tok = lax.broadcasted_iota(jnp.int32, (TM, 1), 0)
    z_left = jnp.where(tok == 0, z_before, pltpu.roll(z, 1, 0))
    z_right = jnp.where(tok == TM - 1, z_after, pltpu.roll(z, TM - 1, 0))
    cw = convw_ref[...]
    conv = z_left * cw[0:1] + z * cw[1:2] + z_right * cw[2:3]
    y_conv = jnp.dot((cb * conv).astype(BF16), wcb_ref[...], preferred_element_type=F32)
    merged = gates[:, :D_MODEL] * y_conv + gated_att
    mixed = jnp.dot(merged.astype(BF16), wout_ref[...], preferred_element_type=F32)
    o_ref[...] = x + _rms(mixed, gpost_ref[...])


def _mix(x, att, gpre, gpost, wc, wgate, bgate, convw, wcb, wab, wout):
    bsz, t, _ = x.shape
    n_tiles = t // TM
    halo_per_tile = TM // CONV_HALO
    n_halo = t // CONV_HALO
    row = pl.BlockSpec((None, TM, D_MODEL), lambda b, i: (b, i, 0))
    prev = pl.BlockSpec((None, CONV_HALO, D_MODEL),
                        lambda b, i: (b, jnp.maximum(i * halo_per_tile - 1, 0), 0))
    nxt = pl.BlockSpec((None, CONV_HALO, D_MODEL),
                       lambda b, i: (b, jnp.minimum((i + 1) * halo_per_tile, n_halo - 1), 0))
    consts = (gpre, gpost, wc, wgate, bgate, convw, wcb, wab, wout)
    return pl.pallas_call(
        _mix_kernel,
        grid=(bsz, n_tiles),
        in_specs=[row, prev, nxt, row] + [_resident(c.shape) for c in consts],
        out_specs=row,
        out_shape=jax.ShapeDtypeStruct(x.shape, x.dtype),
        compiler_params=_params(2),
        name="mix",
    )(x, x, x, att, *consts)


def _ffn_weights(w_in, w_out):
    return w_in.astype(BF16), w_out.astype(BF16)


def _layer(x, p):
    bsz, t, d = x.shape
    flat = lambda a: a.reshape(bsz * t, a.shape[-1])
    x = _ffn(flat(x), p["g1pre"], p["g1post"], *p["ffn1"]).reshape(bsz, t, d)
    q, k, v = _qkv(x, p["gmpre"], p["wqkv"])
    att = _attention(q, k, v, p["bias"])
    x = _mix(x, att, p["gmpre"], p["gmpost"], p["wc"], p["wgate"], p["bgate"], p["convw"],
             p["wcb"], p["wab"], p["wout"])
    x = _ffn(flat(x), p["g2pre"], p["g2post"], *p["ffn2"])
    return x.reshape(bsz, t, d)


def kernel(x_prompt, x_sample, g_ffn1_pre, g_ffn1_post, w_ffn1_in, w_ffn1_out, g_mix_pre, g_mix_post, w_mix_in, b_mix_gate, conv_w, na_rpb, w_conv_branch, w_att_branch, w_mix_out, g_ffn2_pre, g_ffn2_post, w_ffn2_in, w_ffn2_out):
    depth = w_mix_in.shape[0]
    vec = lambda a: a.reshape(1, -1)
    y_prompt, y_sample = x_prompt, x_sample
    for l in range(depth):
        wm = w_mix_in[l]
        p = dict(
            g1pre=vec(g_ffn1_pre[l]), g1post=vec(g_ffn1_post[l]),
            ffn1=_ffn_weights(w_ffn1_in[l], w_ffn1_out[l]),
            gmpre=vec(g_mix_pre[l]), gmpost=vec(g_mix_post[l]),
            wc=wm[:, :3 * D_CONV].astype(BF16),
            wqkv=wm[:, 3 * D_CONV:3 * D_CONV + 3 * D_ATT].astype(BF16),
            wgate=wm[:, 3 * D_CONV + 3 * D_ATT:].astype(BF16),
            bgate=vec(b_mix_gate[l]), convw=conv_w[l],
            bias=_bias_table(na_rpb[l]),
            wcb=w_conv_branch[l].astype(BF16), wab=w_att_branch[l].astype(BF16),
            wout=w_mix_out[l].astype(BF16),
            g2pre=vec(g_ffn2_pre[l]), g2post=vec(g_ffn2_post[l]),
            ffn2=_ffn_weights(w_ffn2_in[l], w_ffn2_out[l]),
        )
        y_prompt = _layer(y_prompt, p)
        y_sample = _layer(y_sample, p)
    return (y_prompt, y_sample)
```

```python
import functools

import jax
import jax.numpy as jnp
from jax import lax
from jax.experimental import pallas as pl
from jax.experimental.pallas import tpu as pltpu

D_MODEL = 1024
GRID_W = 64
D_CONV = 1024
CONV_W = 3
NA_HEADS = 32
NA_HEAD_DIM = 32
D_ATT = NA_HEADS * NA_HEAD_DIM
NA_KH = 8
NA_KW = 16
N_BRANCH = 2
D_FF = 2816
RMS_EPS = 1e-6
NEG_INF = -1e30

LANES = 128
HEADS_PER_GROUP = LANES // NA_HEAD_DIM
N_GROUPS = NA_HEADS // HEADS_PER_GROUP
HALO_ROWS = 4
CONV_HALO = 8

QUARTERS = GRID_W // NA_KW
QCOLS = GRID_W // QUARTERS
assert NA_KH * QCOLS == LANES
MXU_WIDTH = 256
QTILE = MXU_WIDTH // LANES
SUBLANES = 8
PAIR = 16
LOG2E = 1.4426950408889634
Q_SCALE = NA_HEAD_DIM ** -0.5 * LOG2E


def _quarters_seen(q_lo, q_hi):
    seen = set()
    for qc in range(q_lo, q_hi):
        c0 = min(max(qc - NA_KW // 2, 0), GRID_W - NA_KW)
        seen |= {c // QCOLS for c in range(c0, c0 + NA_KW)}
    return list(range(min(seen), max(seen) + 1))


BIAS_TILES = [(j, a) for j in range(GRID_W // SUBLANES)
              for a in _quarters_seen(j * SUBLANES, (j + 1) * SUBLANES)]
BIAS_TILE_INDEX = {ja: n for n, ja in enumerate(BIAS_TILES)}

FFN_CHUNKS = 2
FFN_CHUNK = D_FF // FFN_CHUNKS
TM = 512
FFN_TM = 1024
FFN_SUB = 512
QKV_TM = 1024
QKV_SUB = 512
ATT_ROWS = 16
GROUPS_IN_FLIGHT = 8
ROWS_PER_ITER = 4
VMEM_LIMIT = 56 * 1024 * 1024

F32 = jnp.float32
BF16 = jnp.bfloat16


def _rms(xf, g):
    y = xf * lax.rsqrt(jnp.mean(xf * xf, axis=-1, keepdims=True) + RMS_EPS)
    return y * g


def _resident(shape):
    nd = len(shape)
    return pl.BlockSpec(shape, lambda *_: (0,) * nd, pipeline_mode=pl.Buffered(1))


def _of_layer(layer, block, index=None):
    index = (0,) * len(block) if index is None else index
    return pl.BlockSpec((None,) + tuple(block), lambda *_: (layer,) + tuple(index),
                        pipeline_mode=pl.Buffered(1))


def _params(n_axes):
    return pltpu.CompilerParams(dimension_semantics=("parallel",) * n_axes,
                                vmem_limit_bytes=VMEM_LIMIT)


def _ffn_kernel(x_ref, gpre_ref, gpost_ref, win_ref, wout_ref, o_ref):
    subs = [slice(i * FFN_SUB, (i + 1) * FFN_SUB) for i in range(FFN_TM // FFN_SUB)]
    xn = [None] * len(subs)
    acc = [None] * len(subs)
    for c in range(FFN_CHUNKS):
        cols = slice(c * FFN_CHUNK, (c + 1) * FFN_CHUNK)
        up_cols = slice(D_FF + c * FFN_CHUNK, D_FF + (c + 1) * FFN_CHUNK)
        for i, s in enumerate(subs):
            if c == 0:
                xn[i] = _rms(x_ref[s, :], gpre_ref[...]).astype(BF16)
            g = jnp.dot(xn[i], win_ref[:, cols], preferred_element_type=F32)
            u = jnp.dot(xn[i], win_ref[:, up_cols], preferred_element_type=F32)
            h = (g * jax.nn.sigmoid(g) * u).astype(BF16)
            part = jnp.dot(h, wout_ref[cols, :], preferred_element_type=F32)
            acc[i] = part if acc[i] is None else acc[i] + part
    for i, s in enumerate(subs):
        o_ref[s, :] = x_ref[s, :] + 0.5 * _rms(acc[i], gpost_ref[...])


def _ffn(x, layer, gpre, gpost, w_in, w_out):
    m = x.shape[0]
    row = pl.BlockSpec((FFN_TM, D_MODEL), lambda i: (i, 0))
    return pl.pallas_call(
        _ffn_kernel,
        grid=(m // FFN_TM,),
        in_specs=[row, _of_layer(layer, gpre.shape[1:]), _of_layer(layer, gpost.shape[1:]),
                  _of_layer(layer, w_in.shape[1:]), _of_layer(layer, w_out.shape[1:])],
        out_specs=row,
        out_shape=jax.ShapeDtypeStruct(x.shape, x.dtype),
        compiler_params=_params(1),
        name="ffn",
    )(x, gpre, gpost, w_in, w_out)


def _qkv_kernel(x_ref, g_ref, w_ref, q_ref, k_ref, v_ref):
    n_sub = QKV_TM // QKV_SUB
    ps = [jnp.dot(_rms(x_ref[i * QKV_SUB:(i + 1) * QKV_SUB, :], g_ref[...]).astype(BF16), w_ref[...],
                  preferred_element_type=F32) for i in range(n_sub)]
    for i, p in enumerate(ps):
        q_ref[i * QKV_SUB:(i + 1) * QKV_SUB, :] = (p[:, :D_ATT] * Q_SCALE).astype(BF16)
        for r in range(QKV_SUB // GRID_W):
            for a in range(QUARTERS):
                src = slice(r * GRID_W + a * QCOLS, r * GRID_W + (a + 1) * QCOLS)
                row0 = (i * QKV_SUB // GRID_W + r) * QCOLS
                k_ref[a, row0:row0 + QCOLS, :] = p[src, D_ATT:2 * D_ATT].astype(BF16)
                v_ref[a, row0:row0 + QCOLS, :] = p[src, 2 * D_ATT:].astype(BF16)


def _qkv(x, layer, g, w_mix_in):
    bsz, t, _ = x.shape
    assert (3 * D_CONV) % (3 * D_ATT) == 0
    w_qkv = _of_layer(layer, (D_MODEL, 3 * D_ATT), (0, 3 * D_CONV // (3 * D_ATT)))
    row = pl.BlockSpec((None, QKV_TM, D_MODEL), lambda b, i: (b, i, 0))
    quartered = pl.BlockSpec((None, QUARTERS, QKV_TM // QUARTERS, D_ATT), lambda b, i: (b, 0, i, 0))
    kv = jax.ShapeDtypeStruct((bsz, QUARTERS, t // QUARTERS, D_ATT), BF16)
    return pl.pallas_call(
        _qkv_kernel,
        grid=(bsz, t // QKV_TM),
        in_specs=[row, _of_layer(layer, g.shape[1:]), w_qkv],
        out_specs=[row, quartered, quartered],
        out_shape=[jax.ShapeDtypeStruct((bsz, t, D_ATT), BF16), kv, kv],
        compiler_params=_params(2),
        name="qkv",
    )(x, g, w_mix_in)


def _att_kernel(q_ref, kp_ref, kc_ref, kn_ref, vp_ref, vc_ref, vn_ref, bias_ref, o_ref,
                kbuf, vbuf, *, rows):
    halo = HALO_ROWS * QCOLS
    tile = ATT_ROWS * QCOLS
    kbuf[:, 0:halo] = kp_ref[...]
    kbuf[:, halo:halo + tile] = kc_ref[...]
    kbuf[:, halo + tile:] = kn_ref[...]
    vbuf[:, 0:halo] = vp_ref[...]
    vbuf[:, halo:halo + tile] = vc_ref[...]
    vbuf[:, halo + tile:] = vn_ref[...]

    r0 = pl.program_id(1) * ATT_ROWS
    lane_head = lax.broadcasted_iota(jnp.int32, (PAIR, LANES), 1) // NA_HEAD_DIM
    head_masks = [lane_head == h for h in range(HEADS_PER_GROUP)]
    masked_tile = jnp.full((SUBLANES, LANES), NEG_INF, F32)
    n_pairs = GRID_W // PAIR
    pair_quarters = [_quarters_seen(k * PAIR, (k + 1) * PAIR) for k in range(n_pairs)]
    tile_quarters = [list(range(t * QTILE, (t + 1) * QTILE)) for t in range(QUARTERS // QTILE)]
    tile_pairs = [[k for k in range(n_pairs) if set(pair_quarters[k]) & set(tq)] for tq in tile_quarters]

    def row_stages(j):
        r = r0 + j
        r_start = jnp.clip(r - NA_KH // 2, 0, rows - NA_KH)
        cfg = r - r_start
        woff = pl.multiple_of((r_start - r0 + HALO_ROWS) * QCOLS, QCOLS)
        qoff = pl.multiple_of(j * GRID_W, GRID_W)

        def window(buf, g, quarters):
            return jnp.concatenate([buf[a, pl.ds(woff, LANES), g * LANES:(g + 1) * LANES]
                                    for a in quarters], axis=0)

        def scores(g):
            out = []
            for tq, pairs in zip(tile_quarters, tile_pairs):
                q_lo, n_q = pairs[0] * PAIR, len(pairs) * PAIR
                qg = q_ref[pl.ds(pl.multiple_of(qoff + q_lo, PAIR), n_q), g * LANES:(g + 1) * LANES]
                q_head = lax.broadcasted_iota(jnp.int32, (n_q, LANES), 1) // NA_HEAD_DIM
                qcat = jnp.concatenate(
                    [jnp.where(q_head == h, qg, jnp.zeros_like(qg)) for h in range(HEADS_PER_GROUP)],
                    axis=0)
                out.append(lax.dot_general(qcat, window(kbuf, g, tq), (((1,), (1,)), ((), ())),
                                           preferred_element_type=F32))
            return out

        def softmax(g, s):
            p_rows = [[] for _ in tile_quarters]
            for h in range(HEADS_PER_GROUP):
                head = g * HEADS_PER_GROUP + h
                for k in range(n_pairs):
                    quarters = pair_quarters[k]
                    bias = jnp.concatenate(
                        [jnp.concatenate(
                            [bias_ref[head, cfg, BIAS_TILE_INDEX[(jj, a)]]
                             if (jj, a) in BIAS_TILE_INDEX else masked_tile for a in quarters], axis=1)
                         for jj in range(k * PAIR // SUBLANES, (k + 1) * PAIR // SUBLANES)], axis=0)
                    pieces = []
                    for t, (tq, pairs) in enumerate(zip(tile_quarters, tile_pairs)):
                        seen = [a for a in quarters if a in tq]
                        if seen:
                            row = (h * len(pairs) + k - pairs[0]) * PAIR
                            pieces.append(s[t][row:row + PAIR,
                                               (seen[0] - tq[0]) * LANES:(seen[-1] + 1 - tq[0]) * LANES])
                    x = jnp.concatenate(pieces, axis=1) + bias
                    m = jnp.max(x, axis=-1, keepdims=True)
                    p = jnp.exp2(x - m).astype(BF16)
                    for t, (tq, pairs) in enumerate(zip(tile_quarters, tile_pairs)):
                        if k not in pairs:
                            continue
                        seen = [a for a in quarters if a in tq]
                        parts = [p[:, (seen[0] - quarters[0]) * LANES:(seen[-1] + 1 - quarters[0]) * LANES]]
                        if seen[0] > tq[0]:
                            parts.insert(0, jnp.zeros((PAIR, (seen[0] - tq[0]) * LANES), BF16))
                        if seen[-1] < tq[-1]:
                            parts.append(jnp.zeros((PAIR, (tq[-1] - seen[-1]) * LANES), BF16))
                        p_rows[t].append(jnp.concatenate(parts, axis=1))
            return [jnp.concatenate(rows_t, axis=0) for rows_t in p_rows]

        def output(g, probs):
            ones = jnp.ones((QTILE * LANES, LANES), BF16)
            pv = [jnp.dot(pt, jnp.concatenate([window(vbuf, g, tq), ones], axis=1),
                          preferred_element_type=F32)
                  for pt, tq in zip(probs, tile_quarters)]
            blocks = []
            for k in range(n_pairs):
                o = None
                for h, mk in enumerate(head_masks):
                    acc = None
                    for t, pairs in enumerate(tile_pairs):
                        if k in pairs:
                            row = (h * len(pairs) + k - pairs[0]) * PAIR
                            part = pv[t][row:row + PAIR]
                            acc = part if acc is None else acc + part
                    hk = jnp.where(mk, acc[:, :LANES] / acc[:, LANES:], 0.0)
                    o = hk if o is None else o + hk
                blocks.append(o)
            o_ref[pl.ds(qoff, GRID_W), g * LANES:(g + 1) * LANES] = jnp.concatenate(blocks, axis=0).astype(BF16)

        return scores, softmax, output

    def body(it, carry):
        stages = [row_stages(it * ROWS_PER_ITER + i) for i in range(ROWS_PER_ITER)]
        pending = None
        for i in range(ROWS_PER_ITER):
            scores, softmax, _ = stages[i]
            for g0 in range(0, N_GROUPS, GROUPS_IN_FLIGHT):
                gs = range(g0, g0 + GROUPS_IN_FLIGHT)
                s = [scores(g) for g in gs]
                if pending is not None:
                    for out_fn, g, pg in pending:
                        out_fn(g, pg)
                pending = [(stages[i][2], g, softmax(g, sg)) for g, sg in zip(gs, s)]
        for out_fn, g, pg in pending:
            out_fn(g, pg)
        return carry

    lax.fori_loop(0, ATT_ROWS // ROWS_PER_ITER, body, 0)


def _attention(q, k, v, bias):
    bsz, t, _ = q.shape
    rows = t // GRID_W
    halo_per_tile = ATT_ROWS // HALO_ROWS
    n_halo = rows // HALO_ROWS
    cur_q = pl.BlockSpec((None, ATT_ROWS * GRID_W, D_ATT), lambda b, i: (b, i, 0))
    cur = pl.BlockSpec((None, QUARTERS, ATT_ROWS * QCOLS, D_ATT), lambda b, i: (b, 0, i, 0))
    prev = pl.BlockSpec((None, QUARTERS, HALO_ROWS * QCOLS, D_ATT),
                        lambda b, i: (b, 0, jnp.maximum(i * halo_per_tile - 1, 0), 0))
    nxt = pl.BlockSpec((None, QUARTERS, HALO_ROWS * QCOLS, D_ATT),
                       lambda b, i: (b, 0, jnp.minimum((i + 1) * halo_per_tile, n_halo - 1), 0))
    buf = pltpu.VMEM((QUARTERS, (ATT_ROWS + 2 * HALO_ROWS) * QCOLS, D_ATT), BF16)
    return pl.pallas_call(
        functools.partial(_att_kernel, rows=rows),
        grid=(bsz, rows // ATT_ROWS),
        in_specs=[cur_q, prev, cur, nxt, prev, cur, nxt, _resident(bias.shape)],
        out_specs=cur_q,
        out_shape=jax.ShapeDtypeStruct(q.shape, BF16),
        scratch_shapes=[buf, buf],
        compiler_params=_params(2),
        name="att",
    )(q, k, k, k, v, v, v, bias)


def _bias_table(rpb):
    n_dr = 2 * NA_KH - 1
    r = rpb * LOG2E
    per_query = []
    for j, a in BIAS_TILES:
        for qc in range(j * SUBLANES, (j + 1) * SUBLANES):
            c_start = min(max(qc - NA_KW // 2, 0), GRID_W - NA_KW)
            lo = max(c_start, a * QCOLS)
            hi = min(c_start + NA_KW, (a + 1) * QCOLS)
            if hi <= lo:
                per_query.append(jnp.full((NA_HEADS, n_dr, QCOLS), NEG_INF, F32))
                continue
            dc0 = lo - qc + NA_KW - 1
            run = r[:, :, dc0:dc0 + hi - lo]
            per_query.append(jnp.pad(run, ((0, 0), (0, 0), (lo - a * QCOLS, (a + 1) * QCOLS - hi)),
                                     constant_values=NEG_INF))
    by_dr = jnp.stack(per_query, axis=1).reshape(NA_HEADS, len(per_query), n_dr * QCOLS)
    table = jnp.stack([by_dr[:, :, (NA_KH - 1 - c) * QCOLS:(2 * NA_KH - 1 - c) * QCOLS]
                       for c in range(NA_KH)], axis=1)
    return table.reshape(NA_HEADS, NA_KH, len(BIAS_TILES), SUBLANES, LANES)


def _mix_kernel(x_ref, xp_ref, xn_ref, att_ref, gpre_ref, gpost_ref, wc_ref, wgate_ref, bgate_ref,
                convw_ref, wcb_ref, wab_ref, wout_ref, o_ref):
    i = pl.program_id(1)
    n = pl.num_programs(1)
    x = x_ref[...]
    gpre = gpre_ref[...]
    u_ext = _rms(jnp.concatenate([x, xp_ref[...], xn_ref[...]], axis=0), gpre).astype(BF16)
    u = u_ext[:TM]
    pcv = jnp.dot(u_ext, wc_ref[:, D_CONV:], preferred_element_type=F32)
    z_ext = pcv[:, :D_CONV] * pcv[:, D_CONV:]
    z = z_ext[:TM]
    z_before = jnp.where(i > 0, z_ext[TM + CONV_HALO - 1:TM + CONV_HALO], 0.0)
    z_after = jnp.where(i < n - 1, z_ext[TM + CONV_HALO:TM + CONV_HALO + 1], 0.0)

    y_att = jnp.dot(att_ref[...], wab_ref[...], preferred_element_type=F32)
    gl = jnp.dot(u, wgate_ref[...], preferred_element_type=F32) + bgate_ref[...]
    gates = jax.nn.sigmoid(gl)
    gated_att = gates[:, D_MODEL:] * y_att
    cb = jnp.dot(u, wc_ref[:, :D_CONV], preferred_element_type=F32)

    tok = lax.broadcasted_iota(jnp.int32, (TM, 1), 0)
    z_left = jnp.where(tok == 0, z_before, pltpu.roll(z, 1, 0))
    z_right = jnp.where(tok == TM - 1, z_after, pltpu.roll(z, TM - 1, 0))
    cw = convw_ref[...]
    conv = z_left * cw[0:1] + z * cw[1:2] + z_right * cw[2:3]
    y_conv = jnp.dot((cb * conv).astype(BF16), wcb_ref[...], preferred_element_type=F32)
    merged = gates[:, :D_MODEL] * y_conv + gated_att
    mixed = jnp.dot(merged.astype(BF16), wout_ref[...], preferred_element_type=F32)
    o_ref[...] = x + _rms(mixed, gpost_ref[...])


def _mix(x, att, layer, gpre, gpost, w_mix_in, bgate, convw, wcb, wab, wout):
    bsz, t, _ = x.shape
    n_tiles = t // TM
    halo_per_tile = TM // CONV_HALO
    n_halo = t // CONV_HALO
    row = pl.BlockSpec((None, TM, D_MODEL), lambda b, i: (b, i, 0))
    prev = pl.BlockSpec((None, CONV_HALO, D_MODEL),
                        lambda b, i: (b, jnp.maximum(i * halo_per_tile - 1, 0), 0))
    nxt = pl.BlockSpec((None, CONV_HALO, D_MODEL),
                       lambda b, i: (b, jnp.minimum((i + 1) * halo_per_tile, n_halo - 1), 0))
    gate_cols = N_BRANCH * D_MODEL
    gate_start = 3 * D_CONV + 3 * D_ATT
    assert gate_start % gate_cols == 0
    w_conv = _of_layer(layer, (D_MODEL, 3 * D_CONV), (0, 0))
    w_gate = _of_layer(layer, (D_MODEL, gate_cols), (0, gate_start // gate_cols))
    consts = (gpre, gpost, bgate, convw, wcb, wab, wout)
    gpre_s, gpost_s, bgate_s, convw_s, wcb_s, wab_s, wout_s = [_of_layer(layer, c.shape[1:]) for c in consts]
    return pl.pallas_call(
        _mix_kernel,
        grid=(bsz, n_tiles),
        in_specs=[row, prev, nxt, row, gpre_s, gpost_s, w_conv, w_gate, bgate_s, convw_s, wcb_s, wab_s, wout_s],
        out_specs=row,
        out_shape=jax.ShapeDtypeStruct(x.shape, x.dtype),
        compiler_params=_params(2),
        name="mix",
    )(x, x, x, att, gpre, gpost, w_mix_in, w_mix_in, bgate, convw, wcb, wab, wout)


def _layer(x, layer, p, bias):
    bsz, t, d = x.shape
    flat = lambda a: a.reshape(bsz * t, a.shape[-1])
    x = _ffn(flat(x), layer, p["g1pre"], p["g1post"], p["w1in"], p["w1out"]).reshape(bsz, t, d)
    q, k, v = _qkv(x, layer, p["gmpre"], p["wmix"])
    att = _attention(q, k, v, bias)
    x = _mix(x, att, layer, p["gmpre"], p["gmpost"], p["wmix"], p["bgate"], p["convw"],
             p["wcb"], p["wab"], p["wout"])
    x = _ffn(flat(x), layer, p["g2pre"], p["g2post"], p["w2in"], p["w2out"])
    return x.reshape(bsz, t, d)


def kernel(x_prompt, x_sample, g_ffn1_pre, g_ffn1_post, w_ffn1_in, w_ffn1_out, g_mix_pre, g_mix_post, w_mix_in, b_mix_gate, conv_w, na_rpb, w_conv_branch, w_att_branch, w_mix_out, g_ffn2_pre, g_ffn2_post, w_ffn2_in, w_ffn2_out):
    depth = w_mix_in.shape[0]
    vec = lambda a: a.reshape(depth, 1, -1)
    p = dict(
        g1pre=vec(g_ffn1_pre), g1post=vec(g_ffn1_post),
        w1in=w_ffn1_in.astype(BF16), w1out=w_ffn1_out.astype(BF16),
        gmpre=vec(g_mix_pre), gmpost=vec(g_mix_post), wmix=w_mix_in.astype(BF16),
        bgate=vec(b_mix_gate), convw=conv_w,
        wcb=w_conv_branch.astype(BF16), wab=w_att_branch.astype(BF16), wout=w_mix_out.astype(BF16),
        g2pre=vec(g_ffn2_pre), g2post=vec(g_ffn2_post),
        w2in=w_ffn2_in.astype(BF16), w2out=w_ffn2_out.astype(BF16),
    )
    y_prompt, y_sample = x_prompt, x_sample
    for l in range(depth):
        bias = _bias_table(na_rpb[l])
        y_prompt = _layer(y_prompt, l, p, bias)
        y_sample = _layer(y_sample, l, p, bias)
    return (y_prompt, y_sample)
```

```python
import functools

import jax
import jax.numpy as jnp
from jax import lax
from jax.experimental import pallas as pl
from jax.experimental.pallas import tpu as pltpu

D_MODEL = 1024
GRID_W = 64
D_CONV = 1024
CONV_W = 3
NA_HEADS = 32
NA_HEAD_DIM = 32
D_ATT = NA_HEADS * NA_HEAD_DIM
NA_KH = 8
NA_KW = 16
N_BRANCH = 2
D_FF = 2816
RMS_EPS = 1e-6
NEG_INF = -1e30

LANES = 128
HEADS_PER_GROUP = LANES // NA_HEAD_DIM
N_GROUPS = NA_HEADS // HEADS_PER_GROUP
HALO_ROWS = 4
CONV_HALO = 8

QUARTERS = GRID_W // NA_KW
QCOLS = GRID_W // QUARTERS
assert NA_KH * QCOLS == LANES
MXU_WIDTH = 256
QTILE = MXU_WIDTH // LANES
SUBLANES = 8
PAIR = 16
LOG2E = 1.4426950408889634
Q_SCALE = NA_HEAD_DIM ** -0.5 * LOG2E


def _quarters_seen(q_lo, q_hi):
    seen = set()
    for qc in range(q_lo, q_hi):
        c0 = min(max(qc - NA_KW // 2, 0), GRID_W - NA_KW)
        seen |= {c // QCOLS for c in range(c0, c0 + NA_KW)}
    return list(range(min(seen), max(seen) + 1))


BIAS_TILES = [(j, a) for j in range(GRID_W // SUBLANES)
              for a in _quarters_seen(j * SUBLANES, (j + 1) * SUBLANES)]
BIAS_TILE_INDEX = {ja: n for n, ja in enumerate(BIAS_TILES)}

FFN_CHUNKS = 1
FFN_CHUNK = D_FF // FFN_CHUNKS
assert FFN_CHUNK % MXU_WIDTH == 0
TM = 512
MIX_SUB = 256
FFN_TM = 1024
FFN_SUB = 512
QKV_TM = 1024
QKV_SUB = 512
ATT_ROWS = 16
GROUPS_IN_FLIGHT = 8
ROWS_PER_ITER = 4
VMEM_LIMIT = 56 * 1024 * 1024

F32 = jnp.float32
BF16 = jnp.bfloat16


def _rms(xf, g):
    y = xf * lax.rsqrt(jnp.mean(xf * xf, axis=-1, keepdims=True) + RMS_EPS)
    return y * g


def _resident(shape):
    nd = len(shape)
    return pl.BlockSpec(shape, lambda *_: (0,) * nd, pipeline_mode=pl.Buffered(1))


def _of_layer(layer, block, index=None):
    index = (0,) * len(block) if index is None else index
    return pl.BlockSpec((None,) + tuple(block), lambda *_: (layer,) + tuple(index),
                        pipeline_mode=pl.Buffered(1))


def _params(n_axes):
    return pltpu.CompilerParams(dimension_semantics=("parallel",) * n_axes,
                                vmem_limit_bytes=VMEM_LIMIT)


def _ffn_kernel(x_ref, gpre_ref, gpost_ref, win_ref, wout_ref, o_ref):
    subs = [slice(i * FFN_SUB, (i + 1) * FFN_SUB) for i in range(FFN_TM // FFN_SUB)]
    xn = [None] * len(subs)
    acc = [None] * len(subs)
    for c in range(FFN_CHUNKS):
        cols = slice(c * FFN_CHUNK, (c + 1) * FFN_CHUNK)
        up_cols = slice(D_FF + c * FFN_CHUNK, D_FF + (c + 1) * FFN_CHUNK)
        for i, s in enumerate(subs):
            if c == 0:
                xn[i] = _rms(x_ref[s, :], gpre_ref[...]).astype(BF16)
            g = jnp.dot(xn[i], win_ref[:, cols], preferred_element_type=F32)
            u = jnp.dot(xn[i], win_ref[:, up_cols], preferred_element_type=F32)
            h = (g * jax.nn.sigmoid(g) * u).astype(BF16)
            part = jnp.dot(h, wout_ref[cols, :], preferred_element_type=F32)
            acc[i] = part if acc[i] is None else acc[i] + part
    for i, s in enumerate(subs):
        o_ref[s, :] = x_ref[s, :] + 0.5 * _rms(acc[i], gpost_ref[...])


def _ffn(x, layer, gpre, gpost, w_in, w_out):
    m = x.shape[0]
    row = pl.BlockSpec((FFN_TM, D_MODEL), lambda i: (i, 0))
    return pl.pallas_call(
        _ffn_kernel,
        grid=(m // FFN_TM,),
        in_specs=[row, _of_layer(layer, gpre.shape[1:]), _of_layer(layer, gpost.shape[1:]),
                  _of_layer(layer, w_in.shape[1:]), _of_layer(layer, w_out.shape[1:])],
        out_specs=row,
        out_shape=jax.ShapeDtypeStruct(x.shape, x.dtype),
        compiler_params=_params(1),
        name="ffn",
    )(x, gpre, gpost, w_in, w_out)


def _qkv_kernel(x_ref, g_ref, w_ref, q_ref, k_ref, v_ref):
    n_sub = QKV_TM // QKV_SUB
    ps = [jnp.dot(_rms(x_ref[i * QKV_SUB:(i + 1) * QKV_SUB, :], g_ref[...]).astype(BF16), w_ref[...],
                  preferred_element_type=F32) for i in range(n_sub)]
    for i, p in enumerate(ps):
        q_ref[i * QKV_SUB:(i + 1) * QKV_SUB, :] = (p[:, :D_ATT] * Q_SCALE).astype(BF16)
        for r in range(QKV_SUB // GRID_W):
            for a in range(QUARTERS):
                src = slice(r * GRID_W + a * QCOLS, r * GRID_W + (a + 1) * QCOLS)
                row0 = (i * QKV_SUB // GRID_W + r) * QCOLS
                k_ref[a, row0:row0 + QCOLS, :] = p[src, D_ATT:2 * D_ATT].astype(BF16)
                v_ref[a, row0:row0 + QCOLS, :] = p[src, 2 * D_ATT:].astype(BF16)


def _qkv(x, layer, g, w_mix_in):
    bsz, t, _ = x.shape
    assert (3 * D_CONV) % (3 * D_ATT) == 0
    w_qkv = _of_layer(layer, (D_MODEL, 3 * D_ATT), (0, 3 * D_CONV // (3 * D_ATT)))
    row = pl.BlockSpec((None, QKV_TM, D_MODEL), lambda b, i: (b, i, 0))
    quartered = pl.BlockSpec((None, QUARTERS, QKV_TM // QUARTERS, D_ATT), lambda b, i: (b, 0, i, 0))
    kv = jax.ShapeDtypeStruct((bsz, QUARTERS, t // QUARTERS, D_ATT), BF16)
    return pl.pallas_call(
        _qkv_kernel,
        grid=(bsz, t // QKV_TM),
        in_specs=[row, _of_layer(layer, g.shape[1:]), w_qkv],
        out_specs=[row, quartered, quartered],
        out_shape=[jax.ShapeDtypeStruct((bsz, t, D_ATT), BF16), kv, kv],
        compiler_params=_params(2),
        name="qkv",
    )(x, g, w_mix_in)


def _att_kernel(q_ref, kp_ref, kc_ref, kn_ref, vp_ref, vc_ref, vn_ref, bias_ref, o_ref,
                kbuf, vbuf, *, rows):
    halo = HALO_ROWS * QCOLS
    tile = ATT_ROWS * QCOLS
    kbuf[:, 0:halo] = kp_ref[...]
    kbuf[:, halo:halo + tile] = kc_ref[...]
    kbuf[:, halo + tile:] = kn_ref[...]
    vbuf[:, 0:halo] = vp_ref[...]
    vbuf[:, halo:halo + tile] = vc_ref[...]
    vbuf[:, halo + tile:] = vn_ref[...]

    r0 = pl.program_id(1) * ATT_ROWS
    lane_head = lax.broadcasted_iota(jnp.int32, (PAIR, LANES), 1) // NA_HEAD_DIM
    head_masks = [lane_head == h for h in range(HEADS_PER_GROUP)]
    masked_tile = jnp.full((SUBLANES, LANES), NEG_INF, F32)
    n_pairs = GRID_W // PAIR
    pair_quarters = [_quarters_seen(k * PAIR, (k + 1) * PAIR) for k in range(n_pairs)]
    tile_quarters = [list(range(t * QTILE, (t + 1) * QTILE)) for t in range(QUARTERS // QTILE)]
    tile_pairs = [[k for k in range(n_pairs) if set(pair_quarters[k]) & set(tq)] for tq in tile_quarters]

    def row_stages(j):
        r = r0 + j
        r_start = jnp.clip(r - NA_KH // 2, 0, rows - NA_KH)
        cfg = r - r_start
        woff = pl.multiple_of((r_start - r0 + HALO_ROWS) * QCOLS, QCOLS)
        qoff = pl.multiple_of(j * GRID_W, GRID_W)

        def window(buf, g, quarters):
            return jnp.concatenate([buf[a, pl.ds(woff, LANES), g * LANES:(g + 1) * LANES]
                                    for a in quarters], axis=0)

        def scores(g):
            out = []
            for tq, pairs in zip(tile_quarters, tile_pairs):
                q_lo, n_q = pairs[0] * PAIR, len(pairs) * PAIR
                qg = q_ref[pl.ds(pl.multiple_of(qoff + q_lo, PAIR), n_q), g * LANES:(g + 1) * LANES]
                q_head = lax.broadcasted_iota(jnp.int32, (n_q, LANES), 1) // NA_HEAD_DIM
                qcat = jnp.concatenate(
                    [jnp.where(q_head == h, qg, jnp.zeros_like(qg)) for h in range(HEADS_PER_GROUP)],
                    axis=0)
                out.append(lax.dot_general(qcat, window(kbuf, g, tq), (((1,), (1,)), ((), ())),
                                           preferred_element_type=F32))
            return out

        def softmax(g, s):
            p_rows = [[] for _ in tile_quarters]
            for h in range(HEADS_PER_GROUP):
                head = g * HEADS_PER_GROUP + h
                for k in range(n_pairs):
                    quarters = pair_quarters[k]
                    bias = jnp.concatenate(
                        [jnp.concatenate(
                            [bias_ref[head, cfg, BIAS_TILE_INDEX[(jj, a)]]
                             if (jj, a) in BIAS_TILE_INDEX else masked_tile for a in quarters], axis=1)
                         for jj in range(k * PAIR // SUBLANES, (k + 1) * PAIR // SUBLANES)], axis=0)
                    pieces = []
                    for t, (tq, pairs) in enumerate(zip(tile_quarters, tile_pairs)):
                        seen = [a for a in quarters if a in tq]
                        if seen:
                            row = (h * len(pairs) + k - pairs[0]) * PAIR
                            pieces.append(s[t][row:row + PAIR,
                                               (seen[0] - tq[0]) * LANES:(seen[-1] + 1 - tq[0]) * LANES])
                    x = jnp.concatenate(pieces, axis=1) + bias
                    m = jnp.max(x, axis=-1, keepdims=True)
                    p = jnp.exp2(x - m).astype(BF16)
                    for t, (tq, pairs) in enumerate(zip(tile_quarters, tile_pairs)):
                        if k not in pairs:
                            continue
                        seen = [a for a in quarters if a in tq]
                        parts = [p[:, (seen[0] - quarters[0]) * LANES:(seen[-1] + 1 - quarters[0]) * LANES]]
                        if seen[0] > tq[0]:
                            parts.insert(0, jnp.zeros((PAIR, (seen[0] - tq[0]) * LANES), BF16))
                        if seen[-1] < tq[-1]:
                            parts.append(jnp.zeros((PAIR, (tq[-1] - seen[-1]) * LANES), BF16))
                        p_rows[t].append(jnp.concatenate(parts, axis=1))
            return [jnp.concatenate(rows_t, axis=0) for rows_t in p_rows]

        def output(g, probs):
            ones = jnp.ones((QTILE * LANES, LANES), BF16)
            pv = [jnp.dot(pt, jnp.concatenate([window(vbuf, g, tq), ones], axis=1),
                          preferred_element_type=F32)
                  for pt, tq in zip(probs, tile_quarters)]
            blocks = []
            for k in range(n_pairs):
                o = None
                for h, mk in enumerate(head_masks):
                    acc = None
                    for t, pairs in enumerate(tile_pairs):
                        if k in pairs:
                            row = (h * len(pairs) + k - pairs[0]) * PAIR
                            part = pv[t][row:row + PAIR]
                            acc = part if acc is None else acc + part
                    hk = jnp.where(mk, acc[:, :LANES] / acc[:, LANES:], 0.0)
                    o = hk if o is None else o + hk
                blocks.append(o)
            o_ref[pl.ds(qoff, GRID_W), g * LANES:(g + 1) * LANES] = jnp.concatenate(blocks, axis=0).astype(BF16)

        return scores, softmax, output

    def body(it, carry):
        stages = [row_stages(it * ROWS_PER_ITER + i) for i in range(ROWS_PER_ITER)]
        pending = None
        for i in range(ROWS_PER_ITER):
            scores, softmax, _ = stages[i]
            for g0 in range(0, N_GROUPS, GROUPS_IN_FLIGHT):
                gs = range(g0, g0 + GROUPS_IN_FLIGHT)
                s = [scores(g) for g in gs]
                if pending is not None:
                    for out_fn, g, pg in pending:
                        out_fn(g, pg)
                pending = [(stages[i][2], g, softmax(g, sg)) for g, sg in zip(gs, s)]
        for out_fn, g, pg in pending:
            out_fn(g, pg)
        return carry

    lax.fori_loop(0, ATT_ROWS // ROWS_PER_ITER, body, 0)


def _attention(q, k, v, bias):
    bsz, t, _ = q.shape
    rows = t // GRID_W
    halo_per_tile = ATT_ROWS // HALO_ROWS
    n_halo = rows // HALO_ROWS
    cur_q = pl.BlockSpec((None, ATT_ROWS * GRID_W, D_ATT), lambda b, i: (b, i, 0))
    cur = pl.BlockSpec((None, QUARTERS, ATT_ROWS * QCOLS, D_ATT), lambda b, i: (b, 0, i, 0))
    prev = pl.BlockSpec((None, QUARTERS, HALO_ROWS * QCOLS, D_ATT),
                        lambda b, i: (b, 0, jnp.maximum(i * halo_per_tile - 1, 0), 0))
    nxt = pl.BlockSpec((None, QUARTERS, HALO_ROWS * QCOLS, D_ATT),
                       lambda b, i: (b, 0, jnp.minimum((i + 1) * halo_per_tile, n_halo - 1), 0))
    buf = pltpu.VMEM((QUARTERS, (ATT_ROWS + 2 * HALO_ROWS) * QCOLS, D_ATT), BF16)
    return pl.pallas_call(
        functools.partial(_att_kernel, rows=rows),
        grid=(bsz, rows // ATT_ROWS),
        in_specs=[cur_q, prev, cur, nxt, prev, cur, nxt, _resident(bias.shape)],
        out_specs=cur_q,
        out_shape=jax.ShapeDtypeStruct(q.shape, BF16),
        scratch_shapes=[buf, buf],
        compiler_params=_params(2),
        name="att",
    )(q, k, k, k, v, v, v, bias)


def _bias_table(rpb):
    n_dr = 2 * NA_KH - 1
    r = rpb * LOG2E
    per_query = []
    for j, a in BIAS_TILES:
        for qc in range(j * SUBLANES, (j + 1) * SUBLANES):
            c_start = min(max(qc - NA_KW // 2, 0), GRID_W - NA_KW)
            lo = max(c_start, a * QCOLS)
            hi = min(c_start + NA_KW, (a + 1) * QCOLS)
            if hi <= lo:
                per_query.append(jnp.full((NA_HEADS, n_dr, QCOLS), NEG_INF, F32))
                continue
            dc0 = lo - qc + NA_KW - 1
            run = r[:, :, dc0:dc0 + hi - lo]
            per_query.append(jnp.pad(run, ((0, 0), (0, 0), (lo - a * QCOLS, (a + 1) * QCOLS - hi)),
                                     constant_values=NEG_INF))
    by_dr = jnp.stack(per_query, axis=1).reshape(NA_HEADS, len(per_query), n_dr * QCOLS)
    table = jnp.stack([by_dr[:, :, (NA_KH - 1 - c) * QCOLS:(2 * NA_KH - 1 - c) * QCOLS]
                       for c in range(NA_KH)], axis=1)
    return table.reshape(NA_HEADS, NA_KH, len(BIAS_TILES), SUBLANES, LANES)


def _mix_kernel(x_ref, xp_ref, xn_ref, att_ref, gpre_ref, gpost_ref, wc_ref, wgate_ref, bgate_ref,
                convw_ref, wcb_ref, wab_ref, wout_ref, o_ref):
    i = pl.program_id(1)
    n = pl.num_programs(1)
    gpre = gpre_ref[...]
    subs = [slice(s * MIX_SUB, (s + 1) * MIX_SUB) for s in range(TM // MIX_SUB)]
    u, z_parts = [], []
    for k, s in enumerate(subs):
        xs = x_ref[s, :]
        if k == len(subs) - 1:
            xs = jnp.concatenate([xs, xp_ref[...], xn_ref[...]], axis=0)
        us = _rms(xs, gpre).astype(BF16)
        pcv = jnp.dot(us, wc_ref[:, D_CONV:], preferred_element_type=F32)
        z_parts.append(pcv[:, :D_CONV] * pcv[:, D_CONV:])
        u.append(us[:MIX_SUB])
    z_ext = jnp.concatenate(z_parts, axis=0)
    z = z_ext[:TM]
    z_before = jnp.where(i > 0, z_ext[TM + CONV_HALO - 1:TM + CONV_HALO], 0.0)
    z_after = jnp.where(i < n - 1, z_ext[TM + CONV_HALO:TM + CONV_HALO + 1], 0.0)

    gated_att, gate_conv, cb = [], [], []
    for k, s in enumerate(subs):
        y_att = jnp.dot(att_ref[s, :], wab_ref[...], preferred_element_type=F32)
        gl = jnp.dot(u[k], wgate_ref[...], preferred_element_type=F32) + bgate_ref[...]
        gates = jax.nn.sigmoid(gl)
        gated_att.append(gates[:, D_MODEL:] * y_att)
        gate_conv.append(gates[:, :D_MODEL])
        cb.append(jnp.dot(u[k], wc_ref[:, :D_CONV], preferred_element_type=F32))

    tok = lax.broadcasted_iota(jnp.int32, (TM, 1), 0)
    z_left = jnp.where(tok == 0, z_before, pltpu.roll(z, 1, 0))
    z_right = jnp.where(tok == TM - 1, z_after, pltpu.roll(z, TM - 1, 0))
    cw = convw_ref[...]
    conv = z_left * cw[0:1] + z * cw[1:2] + z_right * cw[2:3]
    mixed = []
    for k, s in enumerate(subs):
        y_conv = jnp.dot((cb[k] * conv[s]).astype(BF16), wcb_ref[...], preferred_element_type=F32)
        merged = gate_conv[k] * y_conv + gated_att[k]
        mixed.append(jnp.dot(merged.astype(BF16), wout_ref[...], preferred_element_type=F32))
    for k, s in enumerate(subs):
        o_ref[s, :] = x_ref[s, :] + _rms(mixed[k], gpost_ref[...])


def _mix(x, att, layer, gpre, gpost, w_mix_in, bgate, convw, wcb, wab, wout):
    bsz, t, _ = x.shape
    n_tiles = t // TM
    halo_per_tile = TM // CONV_HALO
    n_halo = t // CONV_HALO
    row = pl.BlockSpec((None, TM, D_MODEL), lambda b, i: (b, i, 0))
    prev = pl.BlockSpec((None, CONV_HALO, D_MODEL),
                        lambda b, i: (b, jnp.maximum(i * halo_per_tile - 1, 0), 0))
    nxt = pl.BlockSpec((None, CONV_HALO, D_MODEL),
                       lambda b, i: (b, jnp.minimum((i + 1) * halo_per_tile, n_halo - 1), 0))
    gate_cols = N_BRANCH * D_MODEL
    gate_start = 3 * D_CONV + 3 * D_ATT
    assert gate_start % gate_cols == 0
    w_conv = _of_layer(layer, (D_MODEL, 3 * D_CONV), (0, 0))
    w_gate = _of_layer(layer, (D_MODEL, gate_cols), (0, gate_start // gate_cols))
    consts = (gpre, gpost, bgate, convw, wcb, wab, wout)
    gpre_s, gpost_s, bgate_s, convw_s, wcb_s, wab_s, wout_s = [_of_layer(layer, c.shape[1:]) for c in consts]
    return pl.pallas_call(
        _mix_kernel,
        grid=(bsz, n_tiles),
        in_specs=[row, prev, nxt, row, gpre_s, gpost_s, w_conv, w_gate, bgate_s, convw_s, wcb_s, wab_s, wout_s],
        out_specs=row,
        out_shape=jax.ShapeDtypeStruct(x.shape, x.dtype),
        compiler_params=_params(2),
        name="mix",
    )(x, x, x, att, gpre, gpost, w_mix_in, w_mix_in, bgate, convw, wcb, wab, wout)


def _layer(x, layer, p, bias):
    bsz, t, d = x.shape
    flat = lambda a: a.reshape(bsz * t, a.shape[-1])
    x = _ffn(flat(x), layer, p["g1pre"], p["g1post"], p["w1in"], p["w1out"]).reshape(bsz, t, d)
    q, k, v = _qkv(x, layer, p["gmpre"], p["wmix"])
    att = _attention(q, k, v, bias)
    x = _mix(x, att, layer, p["gmpre"], p["gmpost"], p["wmix"], p["bgate"], p["convw"],
             p["wcb"], p["wab"], p["wout"])
    x = _ffn(flat(x), layer, p["g2pre"], p["g2post"], p["w2in"], p["w2out"])
    return x.reshape(bsz, t, d)


def kernel(x_prompt, x_sample, g_ffn1_pre, g_ffn1_post, w_ffn1_in, w_ffn1_out, g_mix_pre, g_mix_post, w_mix_in, b_mix_gate, conv_w, na_rpb, w_conv_branch, w_att_branch, w_mix_out, g_ffn2_pre, g_ffn2_post, w_ffn2_in, w_ffn2_out):
    depth = w_mix_in.shape[0]
    vec = lambda a: a.reshape(depth, 1, -1)
    p = dict(
        g1pre=vec(g_ffn1_pre), g1post=vec(g_ffn1_post),
        w1in=w_ffn1_in.astype(BF16), w1out=w_ffn1_out.astype(BF16),
        gmpre=vec(g_mix_pre), gmpost=vec(g_mix_post), wmix=w_mix_in.astype(BF16),
        bgate=vec(b_mix_gate), convw=conv_w,
        wcb=w_conv_branch.astype(BF16), wab=w_att_branch.astype(BF16), wout=w_mix_out.astype(BF16),
        g2pre=vec(g_ffn2_pre), g2post=vec(g_ffn2_post),
        w2in=w_ffn2_in.astype(BF16), w2out=w_ffn2_out.astype(BF16),
    )
    y_prompt, y_sample = x_prompt, x_sample
    for l in range(depth):
        bias = _bias_table(na_rpb[l])
        y_prompt = _layer(y_prompt, l, p, bias)
        y_sample = _layer(y_sample, l, p, bias)
    return (y_prompt, y_sample)
```

```python
import functools

import jax
import jax.numpy as jnp
from jax import lax
from jax.experimental import pallas as pl
from jax.experimental.pallas import tpu as pltpu

D_MODEL = 1024
GRID_W = 64
D_CONV = 1024
CONV_W = 3
NA_HEADS = 32
NA_HEAD_DIM = 32
D_ATT = NA_HEADS * NA_HEAD_DIM
NA_KH = 8
NA_KW = 16
N_BRANCH = 2
D_FF = 2816
RMS_EPS = 1e-6
NEG_INF = -1e30

LANES = 128
HEADS_PER_GROUP = LANES // NA_HEAD_DIM
N_GROUPS = NA_HEADS // HEADS_PER_GROUP
HALO_ROWS = 4
CONV_HALO = 8

QUARTERS = GRID_W // NA_KW
QCOLS = GRID_W // QUARTERS
assert NA_KH * QCOLS == LANES
MXU_WIDTH = 256
QTILE = MXU_WIDTH // LANES
SUBLANES = 8
PAIR = 16
LOG2E = 1.4426950408889634
Q_SCALE = NA_HEAD_DIM ** -0.5 * LOG2E


def _quarters_seen(q_lo, q_hi):
    seen = set()
    for qc in range(q_lo, q_hi):
        c0 = min(max(qc - NA_KW // 2, 0), GRID_W - NA_KW)
        seen |= {c // QCOLS for c in range(c0, c0 + NA_KW)}
    return list(range(min(seen), max(seen) + 1))


BIAS_TILES = [(j, a) for j in range(GRID_W // SUBLANES)
              for a in _quarters_seen(j * SUBLANES, (j + 1) * SUBLANES)]
BIAS_TILE_INDEX = {ja: n for n, ja in enumerate(BIAS_TILES)}

FFN_CHUNKS = 1
FFN_CHUNK = D_FF // FFN_CHUNKS
assert FFN_CHUNK % MXU_WIDTH == 0
TM = 512
MIX_SUB = 256
FFN_TM = 1024
FFN_SUB = 256
QKV_TM = 1024
QKV_SUB = 512
ATT_ROWS = 16
GROUPS_IN_FLIGHT = 4
ROWS_PER_ITER = 4
VMEM_LIMIT = 56 * 1024 * 1024

F32 = jnp.float32
BF16 = jnp.bfloat16


def _rms(xf, g):
    y = xf * lax.rsqrt(jnp.mean(xf * xf, axis=-1, keepdims=True) + RMS_EPS)
    return y * g


def _resident(shape):
    nd = len(shape)
    return pl.BlockSpec(shape, lambda *_: (0,) * nd, pipeline_mode=pl.Buffered(1))


def _of_layer(layer, block, index=None):
    index = (0,) * len(block) if index is None else index
    return pl.BlockSpec((None,) + tuple(block), lambda *_: (layer,) + tuple(index),
                        pipeline_mode=pl.Buffered(1))


def _params(n_axes):
    return pltpu.CompilerParams(dimension_semantics=("parallel",) * n_axes,
                                vmem_limit_bytes=VMEM_LIMIT)


def _ffn_kernel(x_ref, gpre_ref, gpost_ref, win_ref, wout_ref, o_ref):
    subs = [slice(i * FFN_SUB, (i + 1) * FFN_SUB) for i in range(FFN_TM // FFN_SUB)]
    xn = [None] * len(subs)
    acc = [None] * len(subs)
    for c in range(FFN_CHUNKS):
        cols = slice(c * FFN_CHUNK, (c + 1) * FFN_CHUNK)
        up_cols = slice(D_FF + c * FFN_CHUNK, D_FF + (c + 1) * FFN_CHUNK)
        for i, s in enumerate(subs):
            if c == 0:
                xn[i] = _rms(x_ref[s, :], gpre_ref[...]).astype(BF16)
            g = jnp.dot(xn[i], win_ref[:, cols], preferred_element_type=F32)
            u = jnp.dot(xn[i], win_ref[:, up_cols], preferred_element_type=F32)
            h = (g * jax.nn.sigmoid(g) * u).astype(BF16)
            part = jnp.dot(h, wout_ref[cols, :], preferred_element_type=F32)
            acc[i] = part if acc[i] is None else acc[i] + part
    for i, s in enumerate(subs):
        o_ref[s, :] = x_ref[s, :] + 0.5 * _rms(acc[i], gpost_ref[...])


def _ffn(x, layer, gpre, gpost, w_in, w_out):
    m = x.shape[0]
    row = pl.BlockSpec((FFN_TM, D_MODEL), lambda i: (i, 0))
    return pl.pallas_call(
        _ffn_kernel,
        grid=(m // FFN_TM,),
        in_specs=[row, _of_layer(layer, gpre.shape[1:]), _of_layer(layer, gpost.shape[1:]),
                  _of_layer(layer, w_in.shape[1:]), _of_layer(layer, w_out.shape[1:])],
        out_specs=row,
        out_shape=jax.ShapeDtypeStruct(x.shape, x.dtype),
        compiler_params=_params(1),
        name="ffn",
    )(x, gpre, gpost, w_in, w_out)


def _qkv_kernel(x_ref, g_ref, w_ref, q_ref, k_ref, v_ref):
    n_sub = QKV_TM // QKV_SUB
    ps = [jnp.dot(_rms(x_ref[i * QKV_SUB:(i + 1) * QKV_SUB, :], g_ref[...]).astype(BF16), w_ref[...],
                  preferred_element_type=F32) for i in range(n_sub)]
    for i, p in enumerate(ps):
        q_ref[i * QKV_SUB:(i + 1) * QKV_SUB, :] = (p[:, :D_ATT] * Q_SCALE).astype(BF16)
        for r in range(QKV_SUB // GRID_W):
            for a in range(QUARTERS):
                src = slice(r * GRID_W + a * QCOLS, r * GRID_W + (a + 1) * QCOLS)
                row0 = (i * QKV_SUB // GRID_W + r) * QCOLS
                k_ref[a, row0:row0 + QCOLS, :] = p[src, D_ATT:2 * D_ATT].astype(BF16)
                v_ref[a, row0:row0 + QCOLS, :] = p[src, 2 * D_ATT:].astype(BF16)


def _qkv(x, layer, g, w_mix_in):
    bsz, t, _ = x.shape
    assert (3 * D_CONV) % (3 * D_ATT) == 0
    w_qkv = _of_layer(layer, (D_MODEL, 3 * D_ATT), (0, 3 * D_CONV // (3 * D_ATT)))
    row = pl.BlockSpec((None, QKV_TM, D_MODEL), lambda b, i: (b, i, 0))
    quartered = pl.BlockSpec((None, QUARTERS, QKV_TM // QUARTERS, D_ATT), lambda b, i: (b, 0, i, 0))
    kv = jax.ShapeDtypeStruct((bsz, QUARTERS, t // QUARTERS, D_ATT), BF16)
    return pl.pallas_call(
        _qkv_kernel,
        grid=(bsz, t // QKV_TM),
        in_specs=[row, _of_layer(layer, g.shape[1:]), w_qkv],
        out_specs=[row, quartered, quartered],
        out_shape=[jax.ShapeDtypeStruct((bsz, t, D_ATT), BF16), kv, kv],
        compiler_params=_params(2),
        name="qkv",
    )(x, g, w_mix_in)


def _att_kernel(q_ref, kp_ref, kc_ref, kn_ref, vp_ref, vc_ref, vn_ref, bias_ref, o_ref,
                kbuf, vbuf, *, rows):
    halo = HALO_ROWS * QCOLS
    tile = ATT_ROWS * QCOLS
    kbuf[:, 0:halo] = kp_ref[...]
    kbuf[:, halo:halo + tile] = kc_ref[...]
    kbuf[:, halo + tile:] = kn_ref[...]
    vbuf[:, 0:halo] = vp_ref[...]
    vbuf[:, halo:halo + tile] = vc_ref[...]
    vbuf[:, halo + tile:] = vn_ref[...]

    r0 = pl.program_id(1) * ATT_ROWS
    lane_head = lax.broadcasted_iota(jnp.int32, (PAIR, LANES), 1) // NA_HEAD_DIM
    head_masks = [lane_head == h for h in range(HEADS_PER_GROUP)]
    masked_tile = jnp.full((SUBLANES, LANES), NEG_INF, F32)
    n_pairs = GRID_W // PAIR
    pair_quarters = [_quarters_seen(k * PAIR, (k + 1) * PAIR) for k in range(n_pairs)]
    tile_quarters = [list(range(t * QTILE, (t + 1) * QTILE)) for t in range(QUARTERS // QTILE)]
    tile_pairs = [[k for k in range(n_pairs) if set(pair_quarters[k]) & set(tq)] for tq in tile_quarters]

    def row_stages(j):
        r = r0 + j
        r_start = jnp.clip(r - NA_KH // 2, 0, rows - NA_KH)
        cfg = r - r_start
        woff = pl.multiple_of((r_start - r0 + HALO_ROWS) * QCOLS, QCOLS)
        qoff = pl.multiple_of(j * GRID_W, GRID_W)

        def window(buf, g, quarters):
            return jnp.concatenate([buf[a, pl.ds(woff, LANES), g * LANES:(g + 1) * LANES]
                                    for a in quarters], axis=0)

        def scores(g):
            out = []
            for tq, pairs in zip(tile_quarters, tile_pairs):
                q_lo, n_q = pairs[0] * PAIR, len(pairs) * PAIR
                qg = q_ref[pl.ds(pl.multiple_of(qoff + q_lo, PAIR), n_q), g * LANES:(g + 1) * LANES]
                q_head = lax.broadcasted_iota(jnp.int32, (n_q, LANES), 1) // NA_HEAD_DIM
                qcat = jnp.concatenate(
                    [jnp.where(q_head == h, qg, jnp.zeros_like(qg)) for h in range(HEADS_PER_GROUP)],
                    axis=0)
                out.append(lax.dot_general(qcat, window(kbuf, g, tq), (((1,), (1,)), ((), ())),
                                           preferred_element_type=F32))
            return out

        def softmax(g, s):
            p_rows = [[] for _ in tile_quarters]
            for h in range(HEADS_PER_GROUP):
                head = g * HEADS_PER_GROUP + h
                for k in range(n_pairs):
                    quarters = pair_quarters[k]
                    bias = jnp.concatenate(
                        [jnp.concatenate(
                            [bias_ref[head, cfg, BIAS_TILE_INDEX[(jj, a)]]
                             if (jj, a) in BIAS_TILE_INDEX else masked_tile for a in quarters], axis=1)
                         for jj in range(k * PAIR // SUBLANES, (k + 1) * PAIR // SUBLANES)], axis=0)
                    pieces = []
                    for t, (tq, pairs) in enumerate(zip(tile_quarters, tile_pairs)):
                        seen = [a for a in quarters if a in tq]
                        if seen:
                            row = (h * len(pairs) + k - pairs[0]) * PAIR
                            pieces.append(s[t][row:row + PAIR,
                                               (seen[0] - tq[0]) * LANES:(seen[-1] + 1 - tq[0]) * LANES])
                    x = jnp.concatenate(pieces, axis=1) + bias
                    m = jnp.max(x, axis=-1, keepdims=True)
                    p = jnp.exp2(x - m).astype(BF16)
                    for t, (tq, pairs) in enumerate(zip(tile_quarters, tile_pairs)):
                        if k not in pairs:
                            continue
                        seen = [a for a in quarters if a in tq]
                        parts = [p[:, (seen[0] - quarters[0]) * LANES:(seen[-1] + 1 - quarters[0]) * LANES]]
                        if seen[0] > tq[0]:
                            parts.insert(0, jnp.zeros((PAIR, (seen[0] - tq[0]) * LANES), BF16))
                        if seen[-1] < tq[-1]:
                            parts.append(jnp.zeros((PAIR, (tq[-1] - seen[-1]) * LANES), BF16))
                        p_rows[t].append(jnp.concatenate(parts, axis=1))
            return [jnp.concatenate(rows_t, axis=0) for rows_t in p_rows]

        def output(g, probs):
            ones = jnp.ones((QTILE * LANES, LANES), BF16)
            pv = [jnp.dot(pt, jnp.concatenate([window(vbuf, g, tq), ones], axis=1),
                          preferred_element_type=F32)
                  for pt, tq in zip(probs, tile_quarters)]
            blocks = []
            for k in range(n_pairs):
                o = None
                for h, mk in enumerate(head_masks):
                    acc = None
                    for t, pairs in enumerate(tile_pairs):
                        if k in pairs:
                            row = (h * len(pairs) + k - pairs[0]) * PAIR
                            part = pv[t][row:row + PAIR]
                            acc = part if acc is None else acc + part
                    hk = jnp.where(mk, acc[:, :LANES] / acc[:, LANES:], 0.0)
                    o = hk if o is None else o + hk
                blocks.append(o)
            o_ref[pl.ds(qoff, GRID_W), g * LANES:(g + 1) * LANES] = jnp.concatenate(blocks, axis=0).astype(BF16)

        return scores, softmax, output

    def body(it, carry):
        stages = [row_stages(it * ROWS_PER_ITER + i) for i in range(ROWS_PER_ITER)]
        pending = None
        for i in range(ROWS_PER_ITER):
            scores, softmax, _ = stages[i]
            for g0 in range(0, N_GROUPS, GROUPS_IN_FLIGHT):
                gs = range(g0, g0 + GROUPS_IN_FLIGHT)
                s = [scores(g) for g in gs]
                if pending is not None:
                    for out_fn, g, pg in pending:
                        out_fn(g, pg)
                pending = [(stages[i][2], g, softmax(g, sg)) for g, sg in zip(gs, s)]
        for out_fn, g, pg in pending:
            out_fn(g, pg)
        return carry

    lax.fori_loop(0, ATT_ROWS // ROWS_PER_ITER, body, 0)


def _attention(q, k, v, layer, bias):
    bsz, t, _ = q.shape
    layer_bias = pl.BlockSpec((NA_HEADS,) + bias.shape[1:], lambda *_: (layer,) + (0,) * (bias.ndim - 1),
                              pipeline_mode=pl.Buffered(1))
    rows = t // GRID_W
    halo_per_tile = ATT_ROWS // HALO_ROWS
    n_halo = rows // HALO_ROWS
    cur_q = pl.BlockSpec((None, ATT_ROWS * GRID_W, D_ATT), lambda b, i: (b, i, 0))
    cur = pl.BlockSpec((None, QUARTERS, ATT_ROWS * QCOLS, D_ATT), lambda b, i: (b, 0, i, 0))
    prev = pl.BlockSpec((None, QUARTERS, HALO_ROWS * QCOLS, D_ATT),
                        lambda b, i: (b, 0, jnp.maximum(i * halo_per_tile - 1, 0), 0))
    nxt = pl.BlockSpec((None, QUARTERS, HALO_ROWS * QCOLS, D_ATT),
                       lambda b, i: (b, 0, jnp.minimum((i + 1) * halo_per_tile, n_halo - 1), 0))
    buf = pltpu.VMEM((QUARTERS, (ATT_ROWS + 2 * HALO_ROWS) * QCOLS, D_ATT), BF16)
    return pl.pallas_call(
        functools.partial(_att_kernel, rows=rows),
        grid=(bsz, rows // ATT_ROWS),
        in_specs=[cur_q, prev, cur, nxt, prev, cur, nxt, layer_bias],
        out_specs=cur_q,
        out_shape=jax.ShapeDtypeStruct(q.shape, BF16),
        scratch_shapes=[buf, buf],
        compiler_params=_params(2),
        name="att",
    )(q, k, k, k, v, v, v, bias)


def _bias_table(rpb):
    n_heads = rpb.shape[0]
    n_dr = 2 * NA_KH - 1
    r = rpb * LOG2E
    per_query = []
    for j, a in BIAS_TILES:
        for qc in range(j * SUBLANES, (j + 1) * SUBLANES):
            c_start = min(max(qc - NA_KW // 2, 0), GRID_W - NA_KW)
            lo = max(c_start, a * QCOLS)
            hi = min(c_start + NA_KW, (a + 1) * QCOLS)
            if hi <= lo:
                per_query.append(jnp.full((n_heads, n_dr, QCOLS), NEG_INF, F32))
                continue
            dc0 = lo - qc + NA_KW - 1
            run = r[:, :, dc0:dc0 + hi - lo]
            per_query.append(jnp.pad(run, ((0, 0), (0, 0), (lo - a * QCOLS, (a + 1) * QCOLS - hi)),
                                     constant_values=NEG_INF))
    by_dr = jnp.stack(per_query, axis=1).reshape(n_heads, len(per_query), n_dr * QCOLS)
    table = jnp.stack([by_dr[:, :, (NA_KH - 1 - c) * QCOLS:(2 * NA_KH - 1 - c) * QCOLS]
                       for c in range(NA_KH)], axis=1)
    return table.reshape(n_heads, NA_KH, len(BIAS_TILES), SUBLANES, LANES)


def _mix_kernel(x_ref, xp_ref, xn_ref, att_ref, gpre_ref, gpost_ref, wc_ref, wgate_ref, bgate_ref,
                convw_ref, wcb_ref, wab_ref, wout_ref, o_ref):
    i = pl.program_id(1)
    n = pl.num_programs(1)
    gpre = gpre_ref[...]
    subs = [slice(s * MIX_SUB, (s + 1) * MIX_SUB) for s in range(TM // MIX_SUB)]
    u, z_parts = [], []
    for k, s in enumerate(subs):
        xs = x_ref[s, :]
        if k == len(subs) - 1:
            xs = jnp.concatenate([xs, xp_ref[...], xn_ref[...]], axis=0)
        us = _rms(xs, gpre).astype(BF16)
        pcv = jnp.dot(us, wc_ref[:, D_CONV:], preferred_element_type=F32)
        z_parts.append(pcv[:, :D_CONV] * pcv[:, D_CONV:])
        u.append(us[:MIX_SUB])
    z_ext = jnp.concatenate(z_parts, axis=0)
    z = z_ext[:TM]
    z_before = jnp.where(i > 0, z_ext[TM + CONV_HALO - 1:TM + CONV_HALO], 0.0)
    z_after = jnp.where(i < n - 1, z_ext[TM + CONV_HALO:TM + CONV_HALO + 1], 0.0)

    gated_att, gate_conv, cb = [], [], []
    for k, s in enumerate(subs):
        y_att = jnp.dot(att_ref[s, :], wab_ref[...], preferred_element_type=F32)
        gl = jnp.dot(u[k], wgate_ref[...], preferred_element_type=F32) + bgate_ref[...]
        gates = jax.nn.sigmoid(gl)
        gated_att.append(gates[:, D_MODEL:] * y_att)
        gate_conv.append(gates[:, :D_MODEL])
        cb.append(jnp.dot(u[k], wc_ref[:, :D_CONV], preferred_element_type=F32))

    tok = lax.broadcasted_iota(jnp.int32, (TM, 1), 0)
    z_left = jnp.where(tok == 0, z_before, pltpu.roll(z, 1, 0))
    z_right = jnp.where(tok == TM - 1, z_after, pltpu.roll(z, TM - 1, 0))
    cw = convw_ref[...]
    conv = z_left * cw[0:1] + z * cw[1:2] + z_right * cw[2:3]
    mixed = []
    for k, s in enumerate(subs):
        y_conv = jnp.dot((cb[k] * conv[s]).astype(BF16), wcb_ref[...], preferred_element_type=F32)
        merged = gate_conv[k] * y_conv + gated_att[k]
        mixed.append(jnp.dot(merged.astype(BF16), wout_ref[...], preferred_element_type=F32))
    for k, s in enumerate(subs):
        o_ref[s, :] = x_ref[s, :] + _rms(mixed[k], gpost_ref[...])


def _mix(x, att, layer, gpre, gpost, w_mix_in, bgate, convw, wcb, wab, wout):
    bsz, t, _ = x.shape
    n_tiles = t // TM
    halo_per_tile = TM // CONV_HALO
    n_halo = t // CONV_HALO
    row = pl.BlockSpec((None, TM, D_MODEL), lambda b, i: (b, i, 0))
    prev = pl.BlockSpec((None, CONV_HALO, D_MODEL),
                        lambda b, i: (b, jnp.maximum(i * halo_per_tile - 1, 0), 0))
    nxt = pl.BlockSpec((None, CONV_HALO, D_MODEL),
                       lambda b, i: (b, jnp.minimum((i + 1) * halo_per_tile, n_halo - 1), 0))
    gate_cols = N_BRANCH * D_MODEL
    gate_start = 3 * D_CONV + 3 * D_ATT
    assert gate_start % gate_cols == 0
    w_conv = _of_layer(layer, (D_MODEL, 3 * D_CONV), (0, 0))
    w_gate = _of_layer(layer, (D_MODEL, gate_cols), (0, gate_start // gate_cols))
    consts = (gpre, gpost, bgate, convw, wcb, wab, wout)
    gpre_s, gpost_s, bgate_s, convw_s, wcb_s, wab_s, wout_s = [_of_layer(layer, c.shape[1:]) for c in consts]
    return pl.pallas_call(
        _mix_kernel,
        grid=(bsz, n_tiles),
        in_specs=[row, prev, nxt, row, gpre_s, gpost_s, w_conv, w_gate, bgate_s, convw_s, wcb_s, wab_s, wout_s],
        out_specs=row,
        out_shape=jax.ShapeDtypeStruct(x.shape, x.dtype),
        compiler_params=_params(2),
        name="mix",
    )(x, x, x, att, gpre, gpost, w_mix_in, w_mix_in, bgate, convw, wcb, wab, wout)


def _layer(x, layer, p, bias):
    bsz, t, d = x.shape
    flat = lambda a: a.reshape(bsz * t, a.shape[-1])
    x = _ffn(flat(x), layer, p["g1pre"], p["g1post"], p["w1in"], p["w1out"]).reshape(bsz, t, d)
    q, k, v = _qkv(x, layer, p["gmpre"], p["wmix"])
    att = _attention(q, k, v, layer, bias)
    x = _mix(x, att, layer, p["gmpre"], p["gmpost"], p["wmix"], p["bgate"], p["convw"],
             p["wcb"], p["wab"], p["wout"])
    x = _ffn(flat(x), layer, p["g2pre"], p["g2post"], p["w2in"], p["w2out"])
    return x.reshape(bsz, t, d)


def kernel(x_prompt, x_sample, g_ffn1_pre, g_ffn1_post, w_ffn1_in, w_ffn1_out, g_mix_pre, g_mix_post, w_mix_in, b_mix_gate, conv_w, na_rpb, w_conv_branch, w_att_branch, w_mix_out, g_ffn2_pre, g_ffn2_post, w_ffn2_in, w_ffn2_out):
    depth = w_mix_in.shape[0]
    vec = lambda a: a.reshape(depth, 1, -1)
    p = dict(
        g1pre=vec(g_ffn1_pre), g1post=vec(g_ffn1_post),
        w1in=w_ffn1_in.astype(BF16), w1out=w_ffn1_out.astype(BF16),
        gmpre=vec(g_mix_pre), gmpost=vec(g_mix_post), wmix=w_mix_in.astype(BF16),
        bgate=vec(b_mix_gate), convw=conv_w,
        wcb=w_conv_branch.astype(BF16), wab=w_att_branch.astype(BF16), wout=w_mix_out.astype(BF16),
        g2pre=vec(g_ffn2_pre), g2post=vec(g_ffn2_post),
        w2in=w_ffn2_in.astype(BF16), w2out=w_ffn2_out.astype(BF16),
    )
    bias = _bias_table(na_rpb.reshape((depth * NA_HEADS,) + na_rpb.shape[2:]))
    y_prompt, y_sample = x_prompt, x_sample
    for l in range(depth):
        y_prompt = _layer(y_prompt, l, p, bias)
        y_sample = _layer(y_sample, l, p, bias)
    return (y_prompt, y_sample)
```

```python
import functools

import jax
import jax.numpy as jnp
from jax import lax
from jax.experimental import pallas as pl
from jax.experimental.pallas import tpu as pltpu

D_MODEL = 1024
GRID_W = 64
D_CONV = 1024
CONV_W = 3
NA_HEADS = 32
NA_HEAD_DIM = 32
D_ATT = NA_HEADS * NA_HEAD_DIM
NA_KH = 8
NA_KW = 16
N_BRANCH = 2
D_FF = 2816
RMS_EPS = 1e-6
NEG_INF = -1e30

LANES = 128
HEADS_PER_GROUP = LANES // NA_HEAD_DIM
N_GROUPS = NA_HEADS // HEADS_PER_GROUP
HALO_ROWS = 4
CONV_HALO = 8

QUARTERS = GRID_W // NA_KW
QCOLS = GRID_W // QUARTERS
assert NA_KH * QCOLS == LANES
MXU_WIDTH = 256
QTILE = MXU_WIDTH // LANES
SUBLANES = 8
PAIR = 16
LOG2E = 1.4426950408889634
Q_SCALE = NA_HEAD_DIM ** -0.5 * LOG2E


def _quarters_seen(q_lo, q_hi):
    seen = set()
    for qc in range(q_lo, q_hi):
        c0 = min(max(qc - NA_KW // 2, 0), GRID_W - NA_KW)
        seen |= {c // QCOLS for c in range(c0, c0 + NA_KW)}
    return list(range(min(seen), max(seen) + 1))


BIAS_TILES = [(j, a) for j in range(GRID_W // SUBLANES)
              for a in _quarters_seen(j * SUBLANES, (j + 1) * SUBLANES)]
BIAS_TILE_INDEX = {ja: n for n, ja in enumerate(BIAS_TILES)}

FFN_CHUNKS = 1
FFN_CHUNK = D_FF // FFN_CHUNKS
assert FFN_CHUNK % MXU_WIDTH == 0
TM = 512
MIX_SUB = 256
FFN_TM = 1024
FFN_SUB = 256
QKV_TM = 1024
QKV_SUB = 512
ATT_ROWS = 16
GROUPS_IN_FLIGHT = 4
ROWS_PER_ITER = 4
VMEM_LIMIT = 56 * 1024 * 1024

F32 = jnp.float32
BF16 = jnp.bfloat16


def _rms(xf, g):
    y = xf * lax.rsqrt(jnp.mean(xf * xf, axis=-1, keepdims=True) + RMS_EPS)
    return y * g


def _resident(shape):
    nd = len(shape)
    return pl.BlockSpec(shape, lambda *_: (0,) * nd, pipeline_mode=pl.Buffered(1))


def _of_layer(layer, block, index=None):
    index = (0,) * len(block) if index is None else index
    return pl.BlockSpec((None,) + tuple(block), lambda *_: (layer,) + tuple(index),
                        pipeline_mode=pl.Buffered(1))


def _params(n_axes):
    return pltpu.CompilerParams(dimension_semantics=("parallel",) * n_axes,
                                vmem_limit_bytes=VMEM_LIMIT)


def _ffn_kernel(x_ref, gpre_ref, gpost_ref, win_ref, wout_ref, o_ref):
    subs = [slice(i * FFN_SUB, (i + 1) * FFN_SUB) for i in range(FFN_TM // FFN_SUB)]
    xn = [None] * len(subs)
    acc = [None] * len(subs)
    for c in range(FFN_CHUNKS):
        cols = slice(c * FFN_CHUNK, (c + 1) * FFN_CHUNK)
        up_cols = slice(D_FF + c * FFN_CHUNK, D_FF + (c + 1) * FFN_CHUNK)
        for i, s in enumerate(subs):
            if c == 0:
                xn[i] = _rms(x_ref[s, :], gpre_ref[...]).astype(BF16)
            g = jnp.dot(xn[i], win_ref[:, cols], preferred_element_type=F32)
            u = jnp.dot(xn[i], win_ref[:, up_cols], preferred_element_type=F32)
            h = (g * jax.nn.sigmoid(g) * u).astype(BF16)
            part = jnp.dot(h, wout_ref[cols, :], preferred_element_type=F32)
            acc[i] = part if acc[i] is None else acc[i] + part
    for i, s in enumerate(subs):
        o_ref[s, :] = x_ref[s, :] + 0.5 * _rms(acc[i], gpost_ref[...])


def _ffn(x, layer, gpre, gpost, w_in, w_out):
    m = x.shape[0]
    row = pl.BlockSpec((FFN_TM, D_MODEL), lambda i: (i, 0))
    return pl.pallas_call(
        _ffn_kernel,
        grid=(m // FFN_TM,),
        in_specs=[row, _of_layer(layer, gpre.shape[1:]), _of_layer(layer, gpost.shape[1:]),
                  _of_layer(layer, w_in.shape[1:]), _of_layer(layer, w_out.shape[1:])],
        out_specs=row,
        out_shape=jax.ShapeDtypeStruct(x.shape, x.dtype),
        compiler_params=_params(1),
        name="ffn",
    )(x, gpre, gpost, w_in, w_out)


def _qkv_kernel(x_ref, g_ref, w_ref, q_ref, k_ref, v_ref):
    n_sub = QKV_TM // QKV_SUB
    ps = [jnp.dot(_rms(x_ref[i * QKV_SUB:(i + 1) * QKV_SUB, :], g_ref[...]).astype(BF16), w_ref[...],
                  preferred_element_type=F32) for i in range(n_sub)]
    for i, p in enumerate(ps):
        q_ref[i * QKV_SUB:(i + 1) * QKV_SUB, :] = (p[:, :D_ATT] * Q_SCALE).astype(BF16)
        for r in range(QKV_SUB // GRID_W):
            for a in range(QUARTERS):
                src = slice(r * GRID_W + a * QCOLS, r * GRID_W + (a + 1) * QCOLS)
                row0 = (i * QKV_SUB // GRID_W + r) * QCOLS
                k_ref[a, row0:row0 + QCOLS, :] = p[src, D_ATT:2 * D_ATT].astype(BF16)
                v_ref[a, row0:row0 + QCOLS, :] = p[src, 2 * D_ATT:].astype(BF16)


def _qkv(x, layer, g, w_mix_in):
    bsz, t, _ = x.shape
    assert (3 * D_CONV) % (3 * D_ATT) == 0
    w_qkv = _of_layer(layer, (D_MODEL, 3 * D_ATT), (0, 3 * D_CONV // (3 * D_ATT)))
    row = pl.BlockSpec((None, QKV_TM, D_MODEL), lambda b, i: (b, i, 0))
    quartered = pl.BlockSpec((None, QUARTERS, QKV_TM // QUARTERS, D_ATT), lambda b, i: (b, 0, i, 0))
    kv = jax.ShapeDtypeStruct((bsz, QUARTERS, t // QUARTERS, D_ATT), BF16)
    return pl.pallas_call(
        _qkv_kernel,
        grid=(bsz, t // QKV_TM),
        in_specs=[row, _of_layer(layer, g.shape[1:]), w_qkv],
        out_specs=[row, quartered, quartered],
        out_shape=[jax.ShapeDtypeStruct((bsz, t, D_ATT), BF16), kv, kv],
        compiler_params=_params(2),
        name="qkv",
    )(x, g, w_mix_in)


def _att_kernel(q_ref, kp_ref, kc_ref, kn_ref, vp_ref, vc_ref, vn_ref, bias_ref, o_ref,
                kbuf, vbuf, *, rows):
    halo = HALO_ROWS * QCOLS
    tile = ATT_ROWS * QCOLS
    kbuf[:, 0:halo] = kp_ref[...]
    kbuf[:, halo:halo + tile] = kc_ref[...]
    kbuf[:, halo + tile:] = kn_ref[...]
    vbuf[:, 0:halo] = vp_ref[...]
    vbuf[:, halo:halo + tile] = vc_ref[...]
    vbuf[:, halo + tile:] = vn_ref[...]

    r0 = pl.program_id(1) * ATT_ROWS
    lane_head = lax.broadcasted_iota(jnp.int32, (PAIR, LANES), 1) // NA_HEAD_DIM
    head_masks = [lane_head == h for h in range(HEADS_PER_GROUP)]
    masked_tile = jnp.full((SUBLANES, LANES), NEG_INF, F32)
    n_pairs = GRID_W // PAIR
    pair_quarters = [_quarters_seen(k * PAIR, (k + 1) * PAIR) for k in range(n_pairs)]
    tile_quarters = [list(range(t * QTILE, (t + 1) * QTILE)) for t in range(QUARTERS // QTILE)]
    tile_pairs = [[k for k in range(n_pairs) if set(pair_quarters[k]) & set(tq)] for tq in tile_quarters]

    def row_stages(j):
        r = r0 + j
        r_start = jnp.clip(r - NA_KH // 2, 0, rows - NA_KH)
        cfg = r - r_start
        woff = pl.multiple_of((r_start - r0 + HALO_ROWS) * QCOLS, QCOLS)
        qoff = pl.multiple_of(j * GRID_W, GRID_W)

        def window(buf, g, quarters):
            return jnp.concatenate([buf[a, pl.ds(woff, LANES), g * LANES:(g + 1) * LANES]
                                    for a in quarters], axis=0)

        def scores(g):
            out = []
            for tq, pairs in zip(tile_quarters, tile_pairs):
                q_lo, n_q = pairs[0] * PAIR, len(pairs) * PAIR
                qg = q_ref[pl.ds(pl.multiple_of(qoff + q_lo, PAIR), n_q), g * LANES:(g + 1) * LANES]
                q_head = lax.broadcasted_iota(jnp.int32, (n_q, LANES), 1) // NA_HEAD_DIM
                qcat = jnp.concatenate(
                    [jnp.where(q_head == h, qg, jnp.zeros_like(qg)) for h in range(HEADS_PER_GROUP)],
                    axis=0)
                out.append(lax.dot_general(qcat, window(kbuf, g, tq), (((1,), (1,)), ((), ())),
                                           preferred_element_type=F32))
            return out

        def softmax(g, s):
            p_rows = [[] for _ in tile_quarters]
            for h in range(HEADS_PER_GROUP):
                head = g * HEADS_PER_GROUP + h
                for k in range(n_pairs):
                    quarters = pair_quarters[k]
                    halves = []
                    for jj in range(k * PAIR // SUBLANES, (k + 1) * PAIR // SUBLANES):
                        mine = _quarters_seen(jj * SUBLANES, (jj + 1) * SUBLANES)
                        bias = jnp.concatenate([bias_ref[head, cfg, BIAS_TILE_INDEX[(jj, a)]] for a in mine], axis=1)
                        pieces = []
                        for t, (tq, pairs) in enumerate(zip(tile_quarters, tile_pairs)):
                            seen = [a for a in mine if a in tq]
                            if seen:
                                row = (h * len(pairs) + k - pairs[0]) * PAIR + (jj % 2) * SUBLANES
                                pieces.append(s[t][row:row + SUBLANES,
                                                   (seen[0] - tq[0]) * LANES:(seen[-1] + 1 - tq[0]) * LANES])
                        x = jnp.concatenate(pieces, axis=1) + bias
                        m = jnp.max(x, axis=-1, keepdims=True)
                        parts = [jnp.exp2(x - m)]
                        if mine[0] > quarters[0]:
                            parts.insert(0, jnp.zeros((SUBLANES, (mine[0] - quarters[0]) * LANES), F32))
                        if mine[-1] < quarters[-1]:
                            parts.append(jnp.zeros((SUBLANES, (quarters[-1] - mine[-1]) * LANES), F32))
                        halves.append(jnp.concatenate(parts, axis=1))
                    p = jnp.concatenate(halves, axis=0).astype(BF16)
                    for t, (tq, pairs) in enumerate(zip(tile_quarters, tile_pairs)):
                        if k not in pairs:
                            continue
                        seen = [a for a in quarters if a in tq]
                        parts = [p[:, (seen[0] - quarters[0]) * LANES:(seen[-1] + 1 - quarters[0]) * LANES]]
                        if seen[0] > tq[0]:
                            parts.insert(0, jnp.zeros((PAIR, (seen[0] - tq[0]) * LANES), BF16))
                        if seen[-1] < tq[-1]:
                            parts.append(jnp.zeros((PAIR, (tq[-1] - seen[-1]) * LANES), BF16))
                        p_rows[t].append(jnp.concatenate(parts, axis=1))
            return [jnp.concatenate(rows_t, axis=0) for rows_t in p_rows]

        def output(g, probs):
            ones = jnp.ones((QTILE * LANES, LANES), BF16)
            pv = [jnp.dot(pt, jnp.concatenate([window(vbuf, g, tq), ones], axis=1),
                          preferred_element_type=F32)
                  for pt, tq in zip(probs, tile_quarters)]
            blocks = []
            for k in range(n_pairs):
                o = None
                for h, mk in enumerate(head_masks):
                    acc = None
                    for t, pairs in enumerate(tile_pairs):
                        if k in pairs:
                            row = (h * len(pairs) + k - pairs[0]) * PAIR
                            part = pv[t][row:row + PAIR]
                            acc = part if acc is None else acc + part
                    hk = jnp.where(mk, acc[:, :LANES] / acc[:, LANES:], 0.0)
                    o = hk if o is None else o + hk
                blocks.append(o)
            o_ref[pl.ds(qoff, GRID_W), g * LANES:(g + 1) * LANES] = jnp.concatenate(blocks, axis=0).astype(BF16)

        return scores, softmax, output

    def body(it, carry):
        stages = [row_stages(it * ROWS_PER_ITER + i) for i in range(ROWS_PER_ITER)]
        pending = None
        for i in range(ROWS_PER_ITER):
            scores, softmax, _ = stages[i]
            for g0 in range(0, N_GROUPS, GROUPS_IN_FLIGHT):
                gs = range(g0, g0 + GROUPS_IN_FLIGHT)
                s = [scores(g) for g in gs]
                if pending is not None:
                    for out_fn, g, pg in pending:
                        out_fn(g, pg)
                pending = [(stages[i][2], g, softmax(g, sg)) for g, sg in zip(gs, s)]
        for out_fn, g, pg in pending:
            out_fn(g, pg)
        return carry

    lax.fori_loop(0, ATT_ROWS // ROWS_PER_ITER, body, 0)


def _attention(q, k, v, layer, bias):
    bsz, t, _ = q.shape
    layer_bias = pl.BlockSpec((NA_HEADS,) + bias.shape[1:], lambda *_: (layer,) + (0,) * (bias.ndim - 1),
                              pipeline_mode=pl.Buffered(1))
    rows = t // GRID_W
    halo_per_tile = ATT_ROWS // HALO_ROWS
    n_halo = rows // HALO_ROWS
    cur_q = pl.BlockSpec((None, ATT_ROWS * GRID_W, D_ATT), lambda b, i: (b, i, 0))
    cur = pl.BlockSpec((None, QUARTERS, ATT_ROWS * QCOLS, D_ATT), lambda b, i: (b, 0, i, 0))
    prev = pl.BlockSpec((None, QUARTERS, HALO_ROWS * QCOLS, D_ATT),
                        lambda b, i: (b, 0, jnp.maximum(i * halo_per_tile - 1, 0), 0))
    nxt = pl.BlockSpec((None, QUARTERS, HALO_ROWS * QCOLS, D_ATT),
                       lambda b, i: (b, 0, jnp.minimum((i + 1) * halo_per_tile, n_halo - 1), 0))
    buf = pltpu.VMEM((QUARTERS, (ATT_ROWS + 2 * HALO_ROWS) * QCOLS, D_ATT), BF16)
    return pl.pallas_call(
        functools.partial(_att_kernel, rows=rows),
        grid=(bsz, rows // ATT_ROWS),
        in_specs=[cur_q, prev, cur, nxt, prev, cur, nxt, layer_bias],
        out_specs=cur_q,
        out_shape=jax.ShapeDtypeStruct(q.shape, BF16),
        scratch_shapes=[buf, buf],
        compiler_params=_params(2),
        name="att",
    )(q, k, k, k, v, v, v, bias)


def _bias_table(rpb):
    n_heads = rpb.shape[0]
    n_dr = 2 * NA_KH - 1
    r = rpb * LOG2E
    per_query = []
    for j, a in BIAS_TILES:
        for qc in range(j * SUBLANES, (j + 1) * SUBLANES):
            c_start = min(max(qc - NA_KW // 2, 0), GRID_W - NA_KW)
            lo = max(c_start, a * QCOLS)
            hi = min(c_start + NA_KW, (a + 1) * QCOLS)
            if hi <= lo:
                per_query.append(jnp.full((n_heads, n_dr, QCOLS), NEG_INF, F32))
                continue
            dc0 = lo - qc + NA_KW - 1
            run = r[:, :, dc0:dc0 + hi - lo]
            per_query.append(jnp.pad(run, ((0, 0), (0, 0), (lo - a * QCOLS, (a + 1) * QCOLS - hi)),
                                     constant_values=NEG_INF))
    by_dr = jnp.stack(per_query, axis=1).reshape(n_heads, len(per_query), n_dr * QCOLS)
    table = jnp.stack([by_dr[:, :, (NA_KH - 1 - c) * QCOLS:(2 * NA_KH - 1 - c) * QCOLS]
                       for c in range(NA_KH)], axis=1)
    return table.reshape(n_heads, NA_KH, len(BIAS_TILES), SUBLANES, LANES)


def _mix_kernel(x_ref, xp_ref, xn_ref, att_ref, gpre_ref, gpost_ref, wc_ref, wgate_ref, bgate_ref,
                convw_ref, wcb_ref, wab_ref, wout_ref, o_ref):
    i = pl.program_id(1)
    n = pl.num_programs(1)
    gpre = gpre_ref[...]
    subs = [slice(s * MIX_SUB, (s + 1) * MIX_SUB) for s in range(TM // MIX_SUB)]
    u, z_parts = [], []
    for k, s in enumerate(subs):
        xs = x_ref[s, :]
        if k == len(subs) - 1:
            xs = jnp.concatenate([xs, xp_ref[...], xn_ref[...]], axis=0)
        us = _rms(xs, gpre).astype(BF16)
        pcv = jnp.dot(us, wc_ref[:, D_CONV:], preferred_element_type=F32)
        z_parts.append(pcv[:, :D_CONV] * pcv[:, D_CONV:])
        u.append(us[:MIX_SUB])
    z_ext = jnp.concatenate(z_parts, axis=0)
    z = z_ext[:TM]
    z_before = jnp.where(i > 0, z_ext[TM + CONV_HALO - 1:TM + CONV_HALO], 0.0)
    z_after = jnp.where(i < n - 1, z_ext[TM + CONV_HALO:TM + CONV_HALO + 1], 0.0)

    gated_att, gate_conv, cb = [], [], []
    for k, s in enumerate(subs):
        y_att = jnp.dot(att_ref[s, :], wab_ref[...], preferred_element_type=F32)
        gl = jnp.dot(u[k], wgate_ref[...], preferred_element_type=F32) + bgate_ref[...]
        gates = jax.nn.sigmoid(gl)
        gated_att.append(gates[:, D_MODEL:] * y_att)
        gate_conv.append(gates[:, :D_MODEL])
        cb.append(jnp.dot(u[k], wc_ref[:, :D_CONV], preferred_element_type=F32))

    tok = lax.broadcasted_iota(jnp.int32, (TM, 1), 0)
    z_left = jnp.where(tok == 0, z_before, pltpu.roll(z, 1, 0))
    z_right = jnp.where(tok == TM - 1, z_after, pltpu.roll(z, TM - 1, 0))
    cw = convw_ref[...]
    conv = z_left * cw[0:1] + z * cw[1:2] + z_right * cw[2:3]
    mixed = []
    for k, s in enumerate(subs):
        y_conv = jnp.dot((cb[k] * conv[s]).astype(BF16), wcb_ref[...], preferred_element_type=F32)
        merged = gate_conv[k] * y_conv + gated_att[k]
        mixed.append(jnp.dot(merged.astype(BF16), wout_ref[...], preferred_element_type=F32))
    for k, s in enumerate(subs):
        o_ref[s, :] = x_ref[s, :] + _rms(mixed[k], gpost_ref[...])


def _mix(x, att, layer, gpre, gpost, w_mix_in, bgate, convw, wcb, wab, wout):
    bsz, t, _ = x.shape
    n_tiles = t // TM
    halo_per_tile = TM // CONV_HALO
    n_halo = t // CONV_HALO
    row = pl.BlockSpec((None, TM, D_MODEL), lambda b, i: (b, i, 0))
    prev = pl.BlockSpec((None, CONV_HALO, D_MODEL),
                        lambda b, i: (b, jnp.maximum(i * halo_per_tile - 1, 0), 0))
    nxt = pl.BlockSpec((None, CONV_HALO, D_MODEL),
                       lambda b, i: (b, jnp.minimum((i + 1) * halo_per_tile, n_halo - 1), 0))
    gate_cols = N_BRANCH * D_MODEL
    gate_start = 3 * D_CONV + 3 * D_ATT
    assert gate_start % gate_cols == 0
    w_conv = _of_layer(layer, (D_MODEL, 3 * D_CONV), (0, 0))
    w_gate = _of_layer(layer, (D_MODEL, gate_cols), (0, gate_start // gate_cols))
    consts = (gpre, gpost, bgate, convw, wcb, wab, wout)
    gpre_s, gpost_s, bgate_s, convw_s, wcb_s, wab_s, wout_s = [_of_layer(layer, c.shape[1:]) for c in consts]
    return pl.pallas_call(
        _mix_kernel,
        grid=(bsz, n_tiles),
        in_specs=[row, prev, nxt, row, gpre_s, gpost_s, w_conv, w_gate, bgate_s, convw_s, wcb_s, wab_s, wout_s],
        out_specs=row,
        out_shape=jax.ShapeDtypeStruct(x.shape, x.dtype),
        compiler_params=_params(2),
        name="mix",
    )(x, x, x, att, gpre, gpost, w_mix_in, w_mix_in, bgate, convw, wcb, wab, wout)


def _layer(x, layer, p, bias):
    bsz, t, d = x.shape
    flat = lambda a: a.reshape(bsz * t, a.shape[-1])
    x = _ffn(flat(x), layer, p["g1pre"], p["g1post"], p["w1in"], p["w1out"]).reshape(bsz, t, d)
    q, k, v = _qkv(x, layer, p["gmpre"], p["wmix"])
    att = _attention(q, k, v, layer, bias)
    x = _mix(x, att, layer, p["gmpre"], p["gmpost"], p["wmix"], p["bgate"], p["convw"],
             p["wcb"], p["wab"], p["wout"])
    x = _ffn(flat(x), layer, p["g2pre"], p["g2post"], p["w2in"], p["w2out"])
    return x.reshape(bsz, t, d)


def kernel(x_prompt, x_sample, g_ffn1_pre, g_ffn1_post, w_ffn1_in, w_ffn1_out, g_mix_pre, g_mix_post, w_mix_in, b_mix_gate, conv_w, na_rpb, w_conv_branch, w_att_branch, w_mix_out, g_ffn2_pre, g_ffn2_post, w_ffn2_in, w_ffn2_out):
    depth = w_mix_in.shape[0]
    vec = lambda a: a.reshape(depth, 1, -1)
    p = dict(
        g1pre=vec(g_ffn1_pre), g1post=vec(g_ffn1_post),
        w1in=w_ffn1_in.astype(BF16), w1out=w_ffn1_out.astype(BF16),
        gmpre=vec(g_mix_pre), gmpost=vec(g_mix_post), wmix=w_mix_in.astype(BF16),
        bgate=vec(b_mix_gate), convw=conv_w,
        wcb=w_conv_branch.astype(BF16), wab=w_att_branch.astype(BF16), wout=w_mix_out.astype(BF16),
        g2pre=vec(g_ffn2_pre), g2post=vec(g_ffn2_post),
        w2in=w_ffn2_in.astype(BF16), w2out=w_ffn2_out.astype(BF16),
    )
    bias = _bias_table(na_rpb.reshape((depth * NA_HEADS,) + na_rpb.shape[2:]))
    y_prompt, y_sample = x_prompt, x_sample
    for l in range(depth):
        y_prompt = _layer(y_prompt, l, p, bias)
        y_sample = _layer(y_sample, l, p, bias)
    return (y_prompt, y_sample)
```
